```python
import math
import jax, jax.numpy as jnp
from jax import lax
import numpy as np

D_MODEL = 1024
BATCH = 4
SEQ = 8192
DEPTH = 1

A_HEADS = 4
A_HEAD_DIM = D_MODEL // A_HEADS
A_WIDTH = A_HEADS * A_HEAD_DIM
A_CHUNK = 128
CONV_WIDTH = 4
F_BIAS_LO = 3.0
F_BIAS_HI = 6.0
B_HEAD_DIM = 128
B_HEADS = D_MODEL // B_HEAD_DIM
B_WIDTH = B_HEADS * B_HEAD_DIM
B_BLOCK = 128
NORM_EPS = 1e-6

COLUMN_SIZES = (
    A_WIDTH, A_WIDTH, A_WIDTH,
    A_WIDTH,
    A_WIDTH,
    A_HEADS, A_HEADS,
    B_WIDTH, B_WIDTH, B_WIDTH,
    B_WIDTH,
    D_MODEL, D_MODEL,
)
IN_WIDTH = sum(COLUMN_SIZES)

kernel_name = "hybrid_mlstm_stickbreaking_gated"


def rms_norm(x, gain):
    xf = x.astype(jnp.float32)
    y = xf * lax.rsqrt(jnp.mean(xf * xf, axis=-1, keepdims=True) + NORM_EPS)
    return (y * gain.astype(jnp.float32)).astype(x.dtype)


def split_columns(proj):
    points = [int(p) for p in np.cumsum(np.array(COLUMN_SIZES))[:-1]]
    return jnp.split(proj, points, axis=-1)


def to_heads(t, n_heads):
    b, s, w = t.shape
    return t.reshape(b, s, n_heads, w // n_heads).transpose(0, 2, 1, 3)


def causal_depthwise_conv(x, w, bias):
    k_width, ch = w.shape
    y = lax.conv_general_dilated(
        x, w[:, None, :].astype(x.dtype), window_strides=(1,), padding=[(k_width - 1, 0)],
        dimension_numbers=("NWC", "WIO", "NWC"), feature_group_count=ch)
    return y + bias.astype(x.dtype)


def mlstm_chunkwise(q, k, v, i_pre, f_pre):
    b, h, s, d = q.shape
    L = A_CHUNK
    nc = s // L
    f32 = jnp.float32
    qc = q.astype(f32).reshape(b, h, nc, L, d)
    kc = (k.astype(f32) / math.sqrt(d)).reshape(b, h, nc, L, d)
    vc = v.astype(f32).reshape(b, h, nc, L, d)
    ic = i_pre.reshape(b, h, nc, L)
    logf = jax.nn.log_sigmoid(f_pre).reshape(b, h, nc, L)
    bcum = jnp.cumsum(logf, axis=-1)
    b_last = bcum[..., -1]

    a = b_last[..., None] - bcum + ic
    a_max = jnp.max(a, axis=-1)
    wa = jnp.exp(a - a_max[..., None])
    c_chunk = jnp.einsum("bhcsv,bhcsk->bhcvk", vc * wa[..., None], kc)
    n_chunk = jnp.einsum("bhcs,bhcsk->bhck", wa, kc)

    def step(carry, inp):
        c_state, n_state, m_state = carry
        cc, ncn, amx, bl = inp
        m_new = jnp.maximum(bl + m_state, amx)
        decay = jnp.exp(bl + m_state - m_new)
        scale = jnp.exp(amx - m_new)
        c_new = decay[..., None, None] * c_state + scale[..., None, None] * cc
        n_new = decay[..., None] * n_state + scale[..., None] * ncn
        return (c_new, n_new, m_new), (c_state, n_state, m_state)

    init = (jnp.zeros((b, h, d, d), f32), jnp.zeros((b, h, d), f32), jnp.zeros((b, h), f32))
    xs = (jnp.moveaxis(c_chunk, 2, 0), jnp.moveaxis(n_chunk, 2, 0),
          jnp.moveaxis(a_max, 2, 0), jnp.moveaxis(b_last, 2, 0))
    _, (c_prev, n_prev, m_prev) = lax.scan(step, init, xs)
    c_prev = jnp.moveaxis(c_prev, 0, 2)
    n_prev = jnp.moveaxis(n_prev, 0, 2)
    m_prev = jnp.moveaxis(m_prev, 0, 2)

    causal = jnp.tril(jnp.ones((L, L), dtype=bool))
    log_d = bcum[..., :, None] - bcum[..., None, :] + ic[..., None, :]
    log_d = jnp.where(causal, log_d, -jnp.inf)
    log_inter = bcum + m_prev[..., None]
    m_t = jnp.maximum(log_inter, jnp.max(log_d, axis=-1))
    dmat = jnp.exp(log_d - m_t[..., None])
    inter_w = jnp.exp(log_inter - m_t)
    scores = jnp.einsum("bhctd,bhcsd->bhcts", qc, kc) * dmat
    num = (jnp.einsum("bhcts,bhcsd->bhctd", scores, vc)
           + inter_w[..., None] * jnp.einsum("bhcvk,bhctk->bhctv", c_prev, qc))
    den = jnp.sum(scores, axis=-1) + inter_w * jnp.einsum("bhck,bhctk->bhct", n_prev, qc)
    h_tilde = num / jnp.maximum(jnp.abs(den), jnp.exp(-m_t))[..., None]
    return h_tilde.reshape(b, h, s, d)


def stick_breaking_attention(q, k, v):
    b, h, s, d = q.shape
    scale = 1.0 / math.sqrt(d)
    outs = []
    for blk in range(s // B_BLOCK):
        start = blk * B_BLOCK
        end = start + B_BLOCK
        qb = q[:, :, start:end].astype(jnp.float32)
        kp = k[:, :, :end].astype(jnp.float32)
        vp = v[:, :, :end].astype(jnp.float32)
        z = jnp.einsum("bhqd,bhkd->bhqk", qb, kp) * scale
        mask = jnp.arange(end)[None, :] < jnp.arange(start, end)[:, None]
        log_keep = jnp.where(mask, jax.nn.log_sigmoid(-z), 0.0)
        later = lax.cumsum(log_keep, axis=3, reverse=True) - log_keep
        weights = jnp.where(mask, jnp.exp(jax.nn.log_sigmoid(z) + later), 0.0)
        outs.append(jnp.einsum("bhqk,bhkd->bhqd", weights, vp))
    return jnp.concatenate(outs, axis=2).astype(q.dtype)


def hybrid_layer(x, norm_gain, w_in, conv_w, conv_b, i_bias, f_bias, a_head_gain,
                 qk_gain_q, qk_gain_k, w_branch_a, w_branch_b, w_out):
    bsz, s, _ = x.shape
    hn = rms_norm(x, norm_gain)
    proj = jnp.einsum("bsd,dn->bsn", hn, w_in)
    (qa, ka, va, oa, za, ia, fa, qb, kb, vb, zb, ga, gb) = split_columns(proj)

    qk_a = jax.nn.silu(causal_depthwise_conv(jnp.concatenate([qa, ka], axis=-1), conv_w, conv_b))
    qa_c, ka_c = jnp.split(qk_a, 2, axis=-1)
    i_pre = (ia + i_bias).astype(jnp.float32).transpose(0, 2, 1)
    f_pre = (fa + f_bias).astype(jnp.float32).transpose(0, 2, 1)
    h_tilde = mlstm_chunkwise(to_heads(qa_c, A_HEADS), to_heads(ka_c, A_HEADS),
                              to_heads(va, A_HEADS), i_pre, f_pre)
    h_tilde = h_tilde.transpose(0, 2, 1, 3).astype(x.dtype)
    o_gate = jax.nn.sigmoid(oa).reshape(bsz, s, A_HEADS, A_HEAD_DIM)
    h_a = rms_norm(o_gate * h_tilde, a_head_gain).reshape(bsz, s, A_WIDTH)
    y_a = jnp.einsum("bsw,wd->bsd", h_a * jax.nn.silu(za), w_branch_a)

    q_b = rms_norm(to_heads(qb, B_HEADS), qk_gain_q)
    k_b = rms_norm(to_heads(kb, B_HEADS), qk_gain_k)
    o_b = stick_breaking_attention(q_b, k_b, to_heads(vb, B_HEADS))
    o_b = o_b.transpose(0, 2, 1, 3).reshape(bsz, s, B_WIDTH)
    y_b = jnp.einsum("bsw,wd->bsd", o_b * jax.nn.silu(zb), w_branch_b)

    merged = jax.nn.sigmoid(ga) * y_a + jax.nn.sigmoid(gb) * y_b
    return x + jnp.einsum("bsd,de->bse", merged, w_out)


def setup_inputs(seed: int = 0) -> dict:
    key = jax.random.key(seed)
    ks = jax.random.split(key, 14)
    f32 = jnp.float32
    x = jax.random.normal(ks[0], (BATCH, SEQ, D_MODEL), f32)
    norm_gain = 1.0 + 0.02 * jax.random.normal(ks[1], (DEPTH, D_MODEL), f32)
    w_in = jax.random.normal(ks[2], (DEPTH, D_MODEL, IN_WIDTH), f32) * D_MODEL ** -0.5
    conv_w = jax.random.normal(ks[3], (DEPTH, CONV_WIDTH, 2 * A_WIDTH), f32) * CONV_WIDTH ** -0.5
    conv_b = 0.01 * jax.random.normal(ks[4], (DEPTH, 2 * A_WIDTH), f32)
    i_bias = 0.1 * jax.random.normal(ks[5], (DEPTH, A_HEADS), f32)
    f_bias = (jnp.linspace(F_BIAS_LO, F_BIAS_HI, A_HEADS, dtype=f32)[None, :]
              + 0.1 * jax.random.normal(ks[6], (DEPTH, A_HEADS), f32))
    a_head_gain = 1.0 + 0.02 * jax.random.normal(ks[7], (DEPTH, A_HEADS, A_HEAD_DIM), f32)
    qk_gain_q = 1.0 + 0.02 * jax.random.normal(ks[8], (DEPTH, B_HEAD_DIM), f32)
    qk_gain_k = 1.0 + 0.02 * jax.random.normal(ks[9], (DEPTH, B_HEAD_DIM), f32)
    w_branch_a = jax.random.normal(ks[10], (DEPTH, A_WIDTH, D_MODEL), f32) * A_WIDTH ** -0.5
    w_branch_b = jax.random.normal(ks[11], (DEPTH, B_WIDTH, D_MODEL), f32) * B_WIDTH ** -0.5
    w_out = jax.random.normal(ks[12], (DEPTH, D_MODEL, D_MODEL), f32) * D_MODEL ** -0.5
    return {"x": x, "norm_gain": norm_gain, "w_in": w_in, "conv_w": conv_w, "conv_b": conv_b,
            "i_bias": i_bias, "f_bias": f_bias, "a_head_gain": a_head_gain,
            "qk_gain_q": qk_gain_q, "qk_gain_k": qk_gain_k, "w_branch_a": w_branch_a,
            "w_branch_b": w_branch_b, "w_out": w_out}


def reference(x, norm_gain, w_in, conv_w, conv_b, i_bias, f_bias, a_head_gain,
              qk_gain_q, qk_gain_k, w_branch_a, w_branch_b, w_out):
    for layer in range(DEPTH):
        x = hybrid_layer(x, norm_gain[layer], w_in[layer], conv_w[layer], conv_b[layer],
                         i_bias[layer], f_bias[layer], a_head_gain[layer], qk_gain_q[layer],
                         qk_gain_k[layer], w_branch_a[layer], w_branch_b[layer], w_out[layer])
    return x
```

```python
import functools
import math

import jax
import jax.numpy as jnp
from jax import lax
from jax.experimental import pallas as pl
from jax.experimental.pallas import tpu as pltpu

F32 = jnp.float32
BF16 = jnp.bfloat16

NORM_EPS = 1e-6
A_HEADS = 4
A_CHUNK = 128
CONV_WIDTH = 4
B_HEAD_DIM = 128
LANES = 128
SUBLANES = 8
VMEM_LIMIT = 56 * 1024 * 1024

T_QA, T_KA, T_VA, T_OA, T_ZA, T_QB, T_KB, T_VB, T_ZB, T_GA, T_GB = range(11)
N_TILES = 11
GATE_SLOTS = 16


def _sigmoid(v):
    return 1.0 / (1.0 + jnp.exp(-v))


def _softplus(v):
    return jnp.maximum(v, 0.0) + jnp.log1p(jnp.exp(-jnp.abs(v)))


def _in_proj_body(x_ref, ng_ref, w_ref, wg_ref, cw_ref, cb_ref, qkg_ref, proj_ref, gates_ref,
                  hn_ref, acc_ref, halo_ref, *, tm, d, tiles_per_seq, rc):
    i = pl.program_id(0)
    j = pl.program_id(1)
    n_rc = tm // rc

    @pl.when(j == 0)
    def _():
        def norm_rows(r, carry):
            rows = pl.ds(pl.multiple_of(r * rc, rc), rc)
            xf = x_ref[rows, :]
            ms = jnp.mean(xf * xf, axis=-1, keepdims=True)
            hn_ref[rows, :] = (xf * lax.rsqrt(ms + NORM_EPS) * ng_ref[...]).astype(BF16)
            return carry
        lax.fori_loop(0, n_rc, norm_rows, 0)
        gates_ref[...] = jnp.dot(hn_ref[...], wg_ref[...], preferred_element_type=F32)

    acc = jnp.dot(hn_ref[...], w_ref[...], preferred_element_type=F32)

    is_conv = j <= T_KA
    is_qkn = jnp.logical_or(j == T_QB, j == T_KB)

    @pl.when(is_conv)
    def _():
        first = (i % tiles_per_seq) == 0
        acc_ref[0:SUBLANES, :] = jnp.where(first, 0.0, halo_ref[j])
        acc_ref[SUBLANES:, :] = acc
        halo_ref[j] = acc[tm - SUBLANES:, :]
        post = jnp.where(j == T_KA, 1.0 / math.sqrt(2 * LANES), 1.0)

        def conv_rows(r, carry):
            base = pl.multiple_of(r * rc, rc)
            window = acc_ref[pl.ds(base, rc + SUBLANES), :]
            y = jnp.broadcast_to(cb_ref[...], (rc, d))
            for tap in range(CONV_WIDTH):
                back = CONV_WIDTH - 1 - tap
                shifted = pltpu.roll(window, back, 0) if back else window
                y = y + cw_ref[tap:tap + 1, :] * shifted[SUBLANES:, :]
            proj_ref[pl.ds(base, rc), :] = (y * _sigmoid(y) * post).astype(BF16)
            return carry
        lax.fori_loop(0, n_rc, conv_rows, 0)

    @pl.when(is_qkn)
    def _():
        acc_ref[SUBLANES:, :] = acc

        def qkn_rows(r, carry):
            base = pl.multiple_of(r * rc, rc)
            for hh in range(d // B_HEAD_DIM):
                cols = slice(hh * B_HEAD_DIM, (hh + 1) * B_HEAD_DIM)
                blk = acc_ref[pl.ds(base + SUBLANES, rc), cols]
                ms = jnp.mean(blk * blk, axis=-1, keepdims=True)
                out = blk * lax.rsqrt(ms + NORM_EPS) * qkg_ref[0, :, cols]
                proj_ref[pl.ds(base, rc), cols] = out.astype(BF16)
            return carry
        lax.fori_loop(0, n_rc, qkn_rows, 0)

    @pl.when(jnp.logical_not(jnp.logical_or(is_conv, is_qkn)))
    def _():
        proj_ref[...] = acc.astype(BF16)


def _in_proj(x2d, norm_gain, w_main, w_gate, conv_w, conv_b, qk_gain, *, seq, tm):
    m, d = x2d.shape
    rc = 32
    body = functools.partial(_in_proj_body, tm=tm, d=d, tiles_per_seq=seq // tm, rc=rc)
    return pl.pallas_call(
        body,
        grid=(m // tm, N_TILES),
        in_specs=[
            pl.BlockSpec((tm, d), lambda i, j: (i, 0)),
            pl.BlockSpec((1, d), lambda i, j: (0, 0)),
            pl.BlockSpec((d, d), lambda i, j: (0, j)),
            pl.BlockSpec((d, LANES), lambda i, j: (0, 0)),
            pl.BlockSpec((CONV_WIDTH, d), lambda i, j: (0, jnp.minimum(j, T_KA))),
            pl.BlockSpec((1, d), lambda i, j: (0, jnp.minimum(j, T_KA))),
            pl.BlockSpec((1, 1, d), lambda i, j: (jnp.clip(j - T_QB, 0, 1), 0, 0)),
        ],
        out_specs=[
            pl.BlockSpec((tm, d), lambda i, j: (i, j)),
            pl.BlockSpec((tm, LANES), lambda i, j: (i, 0)),
        ],
        out_shape=[
            jax.ShapeDtypeStruct((m, N_TILES * d), BF16),
            jax.ShapeDtypeStruct((m, LANES), F32),
        ],
        scratch_shapes=[
            pltpu.VMEM((tm, d), BF16),
            pltpu.VMEM((tm + SUBLANES, d), F32),
            pltpu.VMEM((2, SUBLANES, d), F32),
        ],
        compiler_params=pltpu.CompilerParams(
            dimension_semantics=("arbitrary", "arbitrary"), vmem_limit_bytes=VMEM_LIMIT),
        name="in_proj",
    )(x2d, norm_gain, w_main, w_gate, conv_w, conv_b, qk_gain)


def _lane_scan(v, op, ident):
    lane = lax.broadcasted_iota(jnp.int32, v.shape, 1)
    shift = 1
    while shift < v.shape[1]:
        moved = pltpu.roll(v, shift, 1)
        v = op(v, jnp.where(lane >= shift, moved, ident))
        shift *= 2
    return v


def _gates_body(bias_ref, ia_ref, fa_ref, grow_ref, col_ref, mprev_ref, *, nc):
    h = pl.program_id(1)
    L = A_CHUNK
    i_pre = ia_ref[0, 0] + bias_ref[0, h]
    f_pre = fa_ref[0, 0] + bias_ref[1, h]
    logf = jnp.minimum(f_pre, 0.0) - jnp.log1p(jnp.exp(-jnp.abs(f_pre)))
    bcum = _lane_scan(logf, jnp.add, 0.0)
    b_last = jnp.broadcast_to(bcum[:, L - 1:L], (nc, L))
    a = b_last - bcum + i_pre
    a_max = jnp.broadcast_to(jnp.max(a, axis=-1, keepdims=True), (nc, L))
    wa = jnp.exp(a - a_max)
    g = i_pre - bcum
    g_cmax = _lane_scan(g, jnp.maximum, -jnp.inf)

    m_run = jnp.zeros((1, L), F32)
    for c in range(nc):
        mprev_ref[c:c + 1, :] = m_run
        m_run = jnp.maximum(b_last[c:c + 1, :] + m_run, a_max[c:c + 1, :])
    m_prev = mprev_ref[...]

    m_rel = jnp.maximum(m_prev, g_cmax)
    inter_w = jnp.exp(m_prev - m_rel)
    exp_neg_m = jnp.exp(-(bcum + m_rel))
    m_new = jnp.maximum(b_last + m_prev, a_max)
    decay = jnp.exp(b_last + m_prev - m_new)
    scale = jnp.exp(a_max - m_new)

    grow_ref[0, 0] = g
    zeros = jnp.zeros((GATE_SLOTS, L), F32)
    for blk in range(nc // GATE_SLOTS):
        rows = slice(blk * GATE_SLOTS, (blk + 1) * GATE_SLOTS)
        packed = jnp.concatenate(
            [m_rel[rows], inter_w[rows], exp_neg_m[rows], wa[rows], decay[rows], scale[rows], zeros, zeros],
            axis=0)
        col_ref[0, 0, blk] = packed.T


def _mlstm_gates(bias, g8, *, nc):
    b = g8.shape[0]
    L = A_CHUNK
    return pl.pallas_call(
        functools.partial(_gates_body, nc=nc),
        grid=(b, A_HEADS),
        in_specs=[
            pl.BlockSpec(memory_space=pltpu.SMEM),
            pl.BlockSpec((1, 1, nc, L), lambda bi, h: (bi, h, 0, 0)),
            pl.BlockSpec((1, 1, nc, L), lambda bi, h: (bi, A_HEADS + h, 0, 0)),
        ],
        out_specs=[
            pl.BlockSpec((1, 1, nc, L), lambda bi, h: (bi, h, 0, 0)),
            pl.BlockSpec((1, 1, nc // GATE_SLOTS, L, LANES), lambda bi, h: (bi, h, 0, 0, 0)),
        ],
        out_shape=[
            jax.ShapeDtypeStruct((b, A_HEADS, nc, L), F32),
            jax.ShapeDtypeStruct((b, A_HEADS, nc // GATE_SLOTS, L, LANES), F32),
        ],
        scratch_shapes=[pltpu.VMEM((nc, L), F32)],
        compiler_params=pltpu.CompilerParams(dimension_semantics=("parallel", "parallel")),
        name="mlstm_gates",
    )(bias, g8, g8)


def _mlstm_body(q_ref, k_ref, v_ref, o_ref, z_ref, grow_ref, col_ref, gain_ref, out_ref, state_ref, *, hd):
    L = A_CHUNK
    cb = pl.program_id(2)

    @pl.when(cb == 0)
    def _():
        state_ref[...] = jnp.zeros_like(state_ref)

    col = col_ref[0, 0, 0]
    row_i = lax.broadcasted_iota(jnp.int32, (L, L), 0)
    col_i = lax.broadcasted_iota(jnp.int32, (L, L), 1)
    causal = col_i <= row_i
    ones_b = jnp.ones((L, LANES), BF16)
    gain = gain_ref[0]

    for c in range(GATE_SLOTS):
        rows = slice(c * L, (c + 1) * L)

        def colvec(quantity):
            lane = quantity * GATE_SLOTS + c
            return col[:, lane:lane + 1]

        m_rel, inter_w, exp_neg_m, wa = colvec(0), colvec(1), colvec(2), colvec(3)
        decay = colvec(4)[0:1, :]
        scale = colvec(5)[0:1, :]
        g_row = grow_ref[0, 0, c:c + 1, :]

        q = q_ref[rows, :]
        k = k_ref[rows, :]
        v_aug = jnp.concatenate([v_ref[rows, :], ones_b], axis=1)

        dmat = jnp.exp(jnp.where(causal, g_row - m_rel, -jnp.inf))
        scores = lax.dot_general(q, k, (((1,), (1,)), ((), ())), preferred_element_type=F32) * dmat
        state = state_ref[...]
        total = (jnp.dot(scores.astype(BF16), v_aug, preferred_element_type=F32)
                 + inter_w * jnp.dot(q, state.astype(BF16), preferred_element_type=F32))
        num = total[:, :hd]
        den = total[:, hd:]
        inv = 1.0 / jnp.maximum(jnp.abs(den), exp_neg_m)
        h_tilde = num * jnp.concatenate([inv] * (hd // LANES), axis=1)

        gated = _sigmoid(o_ref[rows, :].astype(F32)) * h_tilde
        ms = jnp.mean(gated * gated, axis=-1, keepdims=True)
        h_a = gated * lax.rsqrt(ms + NORM_EPS) * gain
        z = z_ref[rows, :].astype(F32)
        out_ref[rows, :] = (h_a * (z * _sigmoid(z))).astype(BF16)

        kw = (k.astype(F32) * wa).astype(BF16)
        update = lax.dot_general(kw, v_aug, (((0,), (0,)), ((), ())), preferred_element_type=F32)
        state_ref[...] = decay * state + scale * update


def _mlstm_chunks(proj, grow, col, head_gain, *, batch, seq, d):
    hd = d // A_HEADS
    L = A_CHUNK
    rows = GATE_SLOTS * L
    blocks_per_seq = seq // rows
    heads_per_tile = d // hd

    def col_block(tile):
        return pl.BlockSpec((rows, hd), lambda b, h, cb: (b * blocks_per_seq + cb, tile * heads_per_tile + h))

    return pl.pallas_call(
        functools.partial(_mlstm_body, hd=hd),
        grid=(batch, A_HEADS, blocks_per_seq),
        in_specs=[
            col_block(T_QA), col_block(T_KA), col_block(T_VA), col_block(T_OA), col_block(T_ZA),
            pl.BlockSpec((1, 1, GATE_SLOTS, L), lambda b, h, cb: (b, h, cb, 0)),
            pl.BlockSpec((1, 1, 1, L, LANES), lambda b, h, cb: (b, h, cb, 0, 0)),
            pl.BlockSpec((1, 1, hd), lambda b, h, cb: (h, 0, 0)),
        ],
        out_specs=pl.BlockSpec((rows, hd), lambda b, h, cb: (b * blocks_per_seq + cb, h)),
        out_shape=jax.ShapeDtypeStruct((batch * seq, d), BF16),
        scratch_shapes=[pltpu.VMEM((hd, hd + LANES), F32)],
        compiler_params=pltpu.CompilerParams(
            dimension_semantics=("parallel", "parallel", "arbitrary"), vmem_limit_bytes=VMEM_LIMIT),
        name="mlstm_chunks",
    )(proj, proj, proj, proj, proj, grow, col, head_gain)


def _stickbreak_body(q_ref, k_ref, v_ref, out_ref, *, tq, tk):
    qi = pl.program_id(2)
    q = q_ref[...]
    hd = q.shape[1]
    q0 = qi * tq
    n_kb = (qi + 1) * (tq // tk)

    r_i = lax.broadcasted_iota(jnp.int32, (2 * tk, 2 * tk), 0) % tk
    c_i = lax.broadcasted_iota(jnp.int32, (2 * tk, 2 * tk), 1)
    suffix_op = jnp.where(jnp.logical_or(c_i >= tk, r_i > c_i), 1.0, 0.0).astype(BF16)

    def block(kb, carry, masked):
        consumed, acc = carry
        k0 = pl.multiple_of(kb * tk, tk)
        kblk = k_ref[pl.ds(k0, tk), :]
        vblk = v_ref[pl.ds(k0, tk), :]
        z = lax.dot_general(q, kblk, (((1,), (1,)), ((), ())), preferred_element_type=F32)
        sp = _softplus(z)
        if masked:
            t_idx = q0 + lax.broadcasted_iota(jnp.int32, (tq, tk), 0)
            s_idx = k0 + lax.broadcasted_iota(jnp.int32, (tq, tk), 1)
            mask = s_idx < t_idx
            sp = jnp.where(mask, sp, 0.0)
        hi = sp.astype(BF16)
        lo = (sp - hi.astype(F32)).astype(BF16)
        sums = jnp.dot(jnp.concatenate([hi, lo], axis=1), suffix_op, preferred_element_type=F32)
        later = sums[:, :tk] + consumed
        w = jnp.exp(z - sp - later)
        if masked:
            w = jnp.where(mask, w, 0.0)
        acc = acc + jnp.dot(w.astype(BF16), vblk, preferred_element_type=F32)
        return consumed + sums[:, tk:], acc

    carry = (jnp.zeros((tq, tk), F32), jnp.zeros((tq, hd), F32))
    n_diag = tq // tk
    for d_i in range(n_diag):
        carry = block(n_kb - 1 - d_i, carry, True)
    carry = lax.fori_loop(n_diag, n_kb, lambda it, cr: block(n_kb - 1 - it, cr, False), carry)
    out_ref[...] = carry[1].astype(BF16)


def _stickbreak(proj, *, batch, seq, d, tq):
    hd = B_HEAD_DIM
    heads = d // hd
    tk = LANES
    nq = seq // tq
    return pl.pallas_call(
        functools.partial(_stickbreak_body, tq=tq, tk=tk),
        grid=(batch, heads, nq),
        in_specs=[
            pl.BlockSpec((tq, hd), lambda b, h, qi: (b * nq + qi, T_QB * heads + h)),
            pl.BlockSpec((seq, hd), lambda b, h, qi: (b, T_KB * heads + h)),
            pl.BlockSpec((seq, hd), lambda b, h, qi: (b, T_VB * heads + h)),
        ],
        out_specs=pl.BlockSpec((tq, hd), lambda b, h, qi: (b * nq + qi, h)),
        out_shape=jax.ShapeDtypeStruct((batch * seq, d), BF16),
        compiler_params=pltpu.CompilerParams(
            dimension_semantics=("parallel", "parallel", "arbitrary"), vmem_limit_bytes=VMEM_LIMIT),
        name="stickbreak",
    )(proj, proj, proj)


def _out_proj_body(x_ref, ha_ref, ob_ref, zb_ref, ga_ref, gb_ref, wa_ref, wb_ref, wo_ref, out_ref):
    zb = zb_ref[...].astype(F32)
    ob = (ob_ref[...].astype(F32) * (zb * _sigmoid(zb))).astype(BF16)
    y_a = jnp.dot(ha_ref[...], wa_ref[...], preferred_element_type=F32)
    y_b = jnp.dot(ob, wb_ref[...], preferred_element_type=F32)
    merged = _sigmoid(ga_ref[...].astype(F32)) * y_a + _sigmoid(gb_ref[...].astype(F32)) * y_b
    out_ref[...] = x_ref[...] + jnp.dot(merged.astype(BF16), wo_ref[...], preferred_element_type=F32)


def _out_proj(x2d, ha, ob, proj, w_a, w_b, w_o, *, tm):
    m, d = x2d.shape
    row_tile = pl.BlockSpec((tm, d), lambda i: (i, 0))
    weight = pl.BlockSpec((d, d), lambda i: (0, 0))

    def proj_tile(tile):
        return pl.BlockSpec((tm, d), lambda i: (i, tile))

    return pl.pallas_call(
        _out_proj_body,
        grid=(m // tm,),
        in_specs=[row_tile, row_tile, row_tile, proj_tile(T_ZB), proj_tile(T_GA), proj_tile(T_GB),
                  weight, weight, weight],
        out_specs=row_tile,
        out_shape=jax.ShapeDtypeStruct((m, d), F32),
        compiler_params=pltpu.CompilerParams(
            dimension_semantics=("parallel",), vmem_limit_bytes=VMEM_LIMIT),
        name="out_proj",
    )(x2d, ha, ob, proj, proj, proj, w_a, w_b, w_o)


def _layer(x, norm_gain, w_in, conv_w, conv_b, i_bias, f_bias, a_head_gain, qk_gain_q, qk_gain_k,
           w_branch_a, w_branch_b, w_out):
    batch, seq, d = x.shape
    m = batch * seq
    nc = seq // A_CHUNK
    a_w = d
    hd_a = d // A_HEADS
    b_heads = d // B_HEAD_DIM
    assert seq % (GATE_SLOTS * A_CHUNK) == 0 and d % LANES == 0

    gate_lo = 5 * a_w
    gate_hi = gate_lo + 2 * A_HEADS
    w_main = jnp.concatenate([w_in[:, :gate_lo], w_in[:, gate_hi:]], axis=1).astype(BF16)
    w_gate = jnp.pad(w_in[:, gate_lo:gate_hi], ((0, 0), (0, LANES - 2 * A_HEADS))).astype(BF16)
    qk_gain = jnp.stack([jnp.tile(qk_gain_q, b_heads) * (1.0 / math.sqrt(B_HEAD_DIM)),
                         jnp.tile(qk_gain_k, b_heads)])[:, None, :]

    x2d = x.reshape(m, d)
    proj, gates = _in_proj(x2d, norm_gain[None, :], w_main, w_gate, conv_w, conv_b[None, :], qk_gain,
                           seq=seq, tm=min(1024, seq))

    g8 = gates[:, :2 * A_HEADS].reshape(batch, nc, A_CHUNK, 2 * A_HEADS).transpose(0, 3, 1, 2)
    grow, col = _mlstm_gates(jnp.stack([i_bias, f_bias]), g8, nc=nc)
    ha = _mlstm_chunks(proj, grow, col, a_head_gain.reshape(A_HEADS, 1, hd_a), batch=batch, seq=seq, d=d)

    ob = _stickbreak(proj, batch=batch, seq=seq, d=d, tq=256)

    out = _out_proj(x2d, ha, ob, proj, w_branch_a.astype(BF16), w_branch_b.astype(BF16),
                    w_out.astype(BF16), tm=512)
    return out.reshape(batch, seq, d)


def kernel(x, norm_gain, w_in, conv_w, conv_b, i_bias, f_bias, a_head_gain, qk_gain_q, qk_gain_k,
           w_branch_a, w_branch_b, w_out):
    for layer in range(norm_gain.shape[0]):
        x = _layer(x, norm_gain[layer], w_in[layer], conv_w[layer], conv_b[layer], i_bias[layer],
                   f_bias[layer], a_head_gain[layer], qk_gain_q[layer], qk_gain_k[layer],
                   w_branch_a[layer], w_branch_b[layer], w_out[layer])
    return x
```

```python
import functools
import math

import jax
import jax.numpy as jnp
from jax import lax
from jax.experimental import pallas as pl
from jax.experimental.pallas import tpu as pltpu

F32 = jnp.float32
BF16 = jnp.bfloat16

NORM_EPS = 1e-6
A_HEADS = 4
A_CHUNK = 128
CONV_WIDTH = 4
B_HEAD_DIM = 128
LANES = 128
SUBLANES = 8
VMEM_LIMIT = 56 * 1024 * 1024

T_QA, T_KA, T_VA, T_OA, T_ZA, T_QB, T_KB, T_VB, T_ZB, T_GA, T_GB = range(11)
N_TILES = 11
GATE_SLOTS = 16


def _sigmoid(v):
    return 0.5 * (jnp.tanh(0.5 * v) + 1.0)


def _softplus(v):
    return jnp.maximum(v, 0.0) + jnp.log(1.0 + jnp.exp(-jnp.abs(v)))


def _in_proj_body(x_ref, ng_ref, w_ref, wg_ref, cw_ref, cb_ref, qkg_ref, proj_ref, gates_ref,
                  hn_ref, halo_ref, *, tm, d, tiles_per_seq, rc, k_scale):
    i = pl.program_id(0)
    j = pl.program_id(1)

    @pl.when(j == 0)
    def _():
        def norm_rows(r, carry):
            rows = pl.ds(pl.multiple_of(r * rc, rc), rc)
            xf = x_ref[rows, :]
            ms = jnp.mean(xf * xf, axis=-1, keepdims=True)
            hn_ref[rows, :] = (xf * lax.rsqrt(ms + NORM_EPS) * ng_ref[...]).astype(BF16)
            return carry
        lax.fori_loop(0, tm // rc, norm_rows, 0)
        gates_ref[...] = jnp.dot(hn_ref[...], wg_ref[...], preferred_element_type=F32)

    def project():
        return jnp.dot(hn_ref[...], w_ref[...], preferred_element_type=F32)

    is_conv = j <= T_KA
    is_qkn = jnp.logical_or(j == T_QB, j == T_KB)

    @pl.when(is_conv)
    def _():
        acc = project()
        first = (i % tiles_per_seq) == 0
        window = jnp.concatenate([jnp.where(first, 0.0, halo_ref[j]), acc], axis=0)
        halo_ref[j] = acc[tm - SUBLANES:, :]
        y = cw_ref[0:1, :] * window
        for tap in range(1, CONV_WIDTH):
            y = cw_ref[tap:tap + 1, :] * window + pltpu.roll(y, 1, 0)
        y = y[SUBLANES:, :] + cb_ref[...]
        half_post = jnp.where(j == T_KA, 0.5 * k_scale, 0.5)
        proj_ref[...] = (y * half_post * (jnp.tanh(0.5 * y) + 1.0)).astype(BF16)

    @pl.when(is_qkn)
    def _():
        acc = project()
        for hh in range(d // B_HEAD_DIM):
            cols = slice(hh * B_HEAD_DIM, (hh + 1) * B_HEAD_DIM)
            blk = acc[:, cols]
            ms = jnp.mean(blk * blk, axis=-1, keepdims=True)
            proj_ref[:, cols] = (blk * lax.rsqrt(ms + NORM_EPS) * qkg_ref[0, :, cols]).astype(BF16)

    @pl.when(jnp.logical_not(jnp.logical_or(is_conv, is_qkn)))
    def _():
        proj_ref[...] = project().astype(BF16)


def _in_proj(x2d, norm_gain, w_main, w_gate, conv_w, conv_b, qk_gain, *, seq, tm):
    m, d = x2d.shape
    body = functools.partial(_in_proj_body, tm=tm, d=d, tiles_per_seq=seq // tm, rc=128,
                             k_scale=1.0 / math.sqrt(d // A_HEADS))
    return pl.pallas_call(
        body,
        grid=(m // tm, N_TILES),
        in_specs=[
            pl.BlockSpec((tm, d), lambda i, j: (i, 0)),
            pl.BlockSpec((1, d), lambda i, j: (0, 0)),
            pl.BlockSpec((d, d), lambda i, j: (0, j)),
            pl.BlockSpec((d, LANES), lambda i, j: (0, 0)),
            pl.BlockSpec((CONV_WIDTH, d), lambda i, j: (0, jnp.minimum(j, T_KA))),
            pl.BlockSpec((1, d), lambda i, j: (0, jnp.minimum(j, T_KA))),
            pl.BlockSpec((1, 1, d), lambda i, j: (jnp.clip(j - T_QB, 0, 1), 0, 0)),
        ],
        out_specs=[
            pl.BlockSpec((tm, d), lambda i, j: (i, j)),
            pl.BlockSpec((tm, LANES), lambda i, j: (i, 0)),
        ],
        out_shape=[
            jax.ShapeDtypeStruct((m, N_TILES * d), BF16),
            jax.ShapeDtypeStruct((m, LANES), F32),
        ],
        scratch_shapes=[
            pltpu.VMEM((tm, d), BF16),
            pltpu.VMEM((2, SUBLANES, d), F32),
        ],
        compiler_params=pltpu.CompilerParams(
            dimension_semantics=("arbitrary", "arbitrary"), vmem_limit_bytes=VMEM_LIMIT),
        name="in_proj",
    )(x2d, norm_gain, w_main, w_gate, conv_w, conv_b, qk_gain)


def _lane_scan(v, op, ident):
    lane = lax.broadcasted_iota(jnp.int32, v.shape, 1)
    shift = 1
    while shift < v.shape[1]:
        moved = pltpu.roll(v, shift, 1)
        v = op(v, jnp.where(lane >= shift, moved, ident))
        shift *= 2
    return v


def _gates_body(bias_ref, ia_ref, fa_ref, grow_ref, col_ref, mprev_ref, *, nc):
    h = pl.program_id(1)
    L = A_CHUNK
    i_pre = ia_ref[0, 0] + bias_ref[0, h]
    f_pre = fa_ref[0, 0] + bias_ref[1, h]
    logf = jnp.minimum(f_pre, 0.0) - jnp.log1p(jnp.exp(-jnp.abs(f_pre)))
    bcum = _lane_scan(logf, jnp.add, 0.0)
    b_last = jnp.broadcast_to(bcum[:, L - 1:L], (nc, L))
    a = b_last - bcum + i_pre
    a_max = jnp.broadcast_to(jnp.max(a, axis=-1, keepdims=True), (nc, L))
    wa = jnp.exp(a - a_max)
    g = i_pre - bcum
    g_cmax = _lane_scan(g, jnp.maximum, -jnp.inf)

    m_run = jnp.zeros((1, L), F32)
    for c in range(nc):
        mprev_ref[c:c + 1, :] = m_run
        m_run = jnp.maximum(b_last[c:c + 1, :] + m_run, a_max[c:c + 1, :])
    m_prev = mprev_ref[...]

    m_rel = jnp.maximum(m_prev, g_cmax)
    inter_w = jnp.exp(m_prev - m_rel)
    exp_neg_m = jnp.exp(-(bcum + m_rel))
    m_new = jnp.maximum(b_last + m_prev, a_max)
    decay = jnp.exp(b_last + m_prev - m_new)
    scale = jnp.exp(a_max - m_new)

    grow_ref[0, 0] = g
    zeros = jnp.zeros((GATE_SLOTS, L), F32)
    for blk in range(nc // GATE_SLOTS):
        rows = slice(blk * GATE_SLOTS, (blk + 1) * GATE_SLOTS)
        packed = jnp.concatenate(
            [m_rel[rows], inter_w[rows], exp_neg_m[rows], wa[rows], decay[rows], scale[rows], zeros, zeros],
            axis=0)
        col_ref[0, 0, blk] = packed.T


def _mlstm_gates(bias, g8, *, nc):
    b = g8.shape[0]
    L = A_CHUNK
    return pl.pallas_call(
        functools.partial(_gates_body, nc=nc),
        grid=(b, A_HEADS),
        in_specs=[
            pl.BlockSpec(memory_space=pltpu.SMEM),
            pl.BlockSpec((1, 1, nc, L), lambda bi, h: (bi, h, 0, 0)),
            pl.BlockSpec((1, 1, nc, L), lambda bi, h: (bi, A_HEADS + h, 0, 0)),
        ],
        out_specs=[
            pl.BlockSpec((1, 1, nc, L), lambda bi, h: (bi, h, 0, 0)),
            pl.BlockSpec((1, 1, nc // GATE_SLOTS, L, LANES), lambda bi, h: (bi, h, 0, 0, 0)),
        ],
        out_shape=[
            jax.ShapeDtypeStruct((b, A_HEADS, nc, L), F32),
            jax.ShapeDtypeStruct((b, A_HEADS, nc // GATE_SLOTS, L, LANES), F32),
        ],
        scratch_shapes=[pltpu.VMEM((nc, L), F32)],
        compiler_params=pltpu.CompilerParams(dimension_semantics=("parallel", "parallel")),
        name="mlstm_gates",
    )(bias, g8, g8)


def _mlstm_body(q_ref, k_ref, v_ref, o_ref, z_ref, grow_ref, col_ref, gain_ref, out_ref, state_ref, *, hd):
    L = A_CHUNK
    cb = pl.program_id(2)

    @pl.when(cb == 0)
    def _():
        state_ref[...] = jnp.zeros_like(state_ref)

    col = col_ref[0, 0, 0]
    row_i = lax.broadcasted_iota(jnp.int32, (L, L), 0)
    col_i = lax.broadcasted_iota(jnp.int32, (L, L), 1)
    causal = col_i <= row_i
    ones_b = jnp.ones((L, LANES), BF16)
    gain = gain_ref[0]

    for c in range(GATE_SLOTS):
        rows = slice(c * L, (c + 1) * L)

        def colvec(quantity):
            lane = quantity * GATE_SLOTS + c
            return col[:, lane:lane + 1]

        m_rel, inter_w, exp_neg_m, wa = colvec(0), colvec(1), colvec(2), colvec(3)
        decay = colvec(4)[0:1, :]
        scale = colvec(5)[0:1, :]
        g_row = grow_ref[0, 0, c:c + 1, :]

        q = q_ref[rows, :]
        k = k_ref[rows, :]
        v_aug = jnp.concatenate([v_ref[rows, :], ones_b], axis=1)

        dmat = jnp.exp(jnp.where(causal, g_row - m_rel, -jnp.inf))
        scores = lax.dot_general(q, k, (((1,), (1,)), ((), ())), preferred_element_type=F32) * dmat
        state = state_ref[...]
        total = (jnp.dot(scores.astype(BF16), v_aug, preferred_element_type=F32)
                 + inter_w * jnp.dot(q, state.astype(BF16), preferred_element_type=F32))
        num = total[:, :hd]
        den = total[:, hd:]
        inv = 1.0 / jnp.maximum(jnp.abs(den), exp_neg_m)
        h_tilde = num * jnp.concatenate([inv] * (hd // LANES), axis=1)

        gated = _sigmoid(o_ref[rows, :].astype(F32)) * h_tilde
        ms = jnp.mean(gated * gated, axis=-1, keepdims=True)
        h_a = gated * lax.rsqrt(ms + NORM_EPS) * gain
        z = z_ref[rows, :].astype(F32)
        out_ref[rows, :] = (h_a * (z * _sigmoid(z))).astype(BF16)

        kw = (k.astype(F32) * wa).astype(BF16)
        update = lax.dot_general(kw, v_aug, (((0,), (0,)), ((), ())), preferred_element_type=F32)
        state_ref[...] = decay * state + scale * update


def _mlstm_chunks(proj, grow, col, head_gain, *, batch, seq, d):
    hd = d // A_HEADS
    L = A_CHUNK
    rows = GATE_SLOTS * L
    blocks_per_seq = seq // rows
    heads_per_tile = d // hd

    def col_block(tile):
        return pl.BlockSpec((rows, hd), lambda b, h, cb: (b * blocks_per_seq + cb, tile * heads_per_tile + h))

    return pl.pallas_call(
        functools.partial(_mlstm_body, hd=hd),
        grid=(batch, A_HEADS, blocks_per_seq),
        in_specs=[
            col_block(T_QA), col_block(T_KA), col_block(T_VA), col_block(T_OA), col_block(T_ZA),
            pl.BlockSpec((1, 1, GATE_SLOTS, L), lambda b, h, cb: (b, h, cb, 0)),
            pl.BlockSpec((1, 1, 1, L, LANES), lambda b, h, cb: (b, h, cb, 0, 0)),
            pl.BlockSpec((1, 1, hd), lambda b, h, cb: (h, 0, 0)),
        ],
        out_specs=pl.BlockSpec((rows, hd), lambda b, h, cb: (b * blocks_per_seq + cb, h)),
        out_shape=jax.ShapeDtypeStruct((batch * seq, d), BF16),
        scratch_shapes=[pltpu.VMEM((hd, hd + LANES), F32)],
        compiler_params=pltpu.CompilerParams(
            dimension_semantics=("parallel", "parallel", "arbitrary"), vmem_limit_bytes=VMEM_LIMIT),
        name="mlstm_chunks",
    )(proj, proj, proj, proj, proj, grow, col, head_gain)


def _stickbreak_body(q_ref, k_ref, v_ref, out_ref, *, tq, tk):
    qi = pl.program_id(2)
    q = q_ref[...]
    hd = q.shape[1]
    q0 = qi * tq
    n_sub = tq // tk

    r_i = lax.broadcasted_iota(jnp.int32, (2 * tk, 2 * tk), 0) % tk
    c_i = lax.broadcasted_iota(jnp.int32, (2 * tk, 2 * tk), 1)
    suffix_op = jnp.where(jnp.logical_or(c_i >= tk, r_i > c_i), 1.0, 0.0).astype(BF16)

    def span(k0, carry, masked):
        consumed, acc = carry
        kblk = k_ref[pl.ds(k0, tq), :]
        vblk = v_ref[pl.ds(k0, tq), :]
        z = lax.dot_general(q, kblk, (((1,), (1,)), ((), ())), preferred_element_type=F32)
        sp = _softplus(z)
        if masked:
            t_idx = q0 + lax.broadcasted_iota(jnp.int32, (tq, tq), 0)
            s_idx = k0 + lax.broadcasted_iota(jnp.int32, (tq, tq), 1)
            mask = s_idx < t_idx
            sp = jnp.where(mask, sp, 0.0)
        hi = sp.astype(BF16)
        lo = (sp - hi.astype(F32)).astype(BF16)
        log_beta = z - sp
        w_parts = [None] * n_sub
        for blk in reversed(range(n_sub)):
            cols = slice(blk * tk, (blk + 1) * tk)
            sums = jnp.dot(jnp.concatenate([hi[:, cols], lo[:, cols]], axis=1), suffix_op,
                           preferred_element_type=F32)
            w_parts[blk] = jnp.exp(log_beta[:, cols] - (sums[:, :tk] + consumed))
            consumed = consumed + sums[:, tk:]
        w = jnp.concatenate(w_parts, axis=1)
        if masked:
            w = jnp.where(mask, w, 0.0)
        acc = acc + jnp.dot(w.astype(BF16), vblk, preferred_element_type=F32)
        return consumed, acc

    carry = (jnp.zeros((tq, tk), F32), jnp.zeros((tq, hd), F32))
    carry = span(pl.multiple_of(q0, tq), carry, True)
    carry = lax.fori_loop(
        0, qi, lambda it, cr: span(pl.multiple_of((qi - 1 - it) * tq, tq), cr, False), carry)
    out_ref[...] = carry[1].astype(BF16)


def _stickbreak(proj, *, batch, seq, d, tq):
    hd = B_HEAD_DIM
    heads = d // hd
    tk = LANES
    nq = seq // tq
    return pl.pallas_call(
        functools.partial(_stickbreak_body, tq=tq, tk=tk),
        grid=(batch, heads, nq),
        in_specs=[
            pl.BlockSpec((tq, hd), lambda b, h, qi: (b * nq + qi, T_QB * heads + h)),
            pl.BlockSpec((seq, hd), lambda b, h, qi: (b, T_KB * heads + h)),
            pl.BlockSpec((seq, hd), lambda b, h, qi: (b, T_VB * heads + h)),
        ],
        out_specs=pl.BlockSpec((tq, hd), lambda b, h, qi: (b * nq + qi, h)),
        out_shape=jax.ShapeDtypeStruct((batch * seq, d), BF16),
        compiler_params=pltpu.CompilerParams(
            dimension_semantics=("parallel", "parallel", "arbitrary"), vmem_limit_bytes=VMEM_LIMIT),
        name="stickbreak",
    )(proj, proj, proj)


def _out_proj_body(x_ref, ha_ref, ob_ref, zb_ref, ga_ref, gb_ref, wa_ref, wb_ref, wo_ref, out_ref):
    zb = zb_ref[...].astype(F32)
    ob = (ob_ref[...].astype(F32) * (zb * _sigmoid(zb))).astype(BF16)
    y_a = jnp.dot(ha_ref[...], wa_ref[...], preferred_element_type=F32)
    y_b = jnp.dot(ob, wb_ref[...], preferred_element_type=F32)
    merged = _sigmoid(ga_ref[...].astype(F32)) * y_a + _sigmoid(gb_ref[...].astype(F32)) * y_b
    out_ref[...] = x_ref[...] + jnp.dot(merged.astype(BF16), wo_ref[...], preferred_element_type=F32)


def _out_proj(x2d, ha, ob, proj, w_a, w_b, w_o, *, tm):
    m, d = x2d.shape
    row_tile = pl.BlockSpec((tm, d), lambda i: (i, 0))
    weight = pl.BlockSpec((d, d), lambda i: (0, 0))

    def proj_tile(tile):
        return pl.BlockSpec((tm, d), lambda i: (i, tile))

    return pl.pallas_call(
        _out_proj_body,
        grid=(m // tm,),
        in_specs=[row_tile, row_tile, row_tile, proj_tile(T_ZB), proj_tile(T_GA), proj_tile(T_GB),
                  weight, weight, weight],
        out_specs=row_tile,
        out_shape=jax.ShapeDtypeStruct((m, d), F32),
        compiler_params=pltpu.CompilerParams(
            dimension_semantics=("parallel",), vmem_limit_bytes=VMEM_LIMIT),
        name="out_proj",
    )(x2d, ha, ob, proj, proj, proj, w_a, w_b, w_o)


def _layer(x, norm_gain, w_in, conv_w, conv_b, i_bias, f_bias, a_head_gain, qk_gain_q, qk_gain_k,
           w_branch_a, w_branch_b, w_out):
    batch, seq, d = x.shape
    m = batch * seq
    nc = seq // A_CHUNK
    a_w = d
    hd_a = d // A_HEADS
    b_heads = d // B_HEAD_DIM
    assert seq % (GATE_SLOTS * A_CHUNK) == 0 and d % LANES == 0

    gate_lo = 5 * a_w
    gate_hi = gate_lo + 2 * A_HEADS
    w_main = jnp.concatenate([w_in[:, :gate_lo], w_in[:, gate_hi:]], axis=1).astype(BF16)
    w_gate = jnp.pad(w_in[:, gate_lo:gate_hi], ((0, 0), (0, LANES - 2 * A_HEADS))).astype(BF16)
    qk_gain = jnp.stack([jnp.tile(qk_gain_q, b_heads) * (1.0 / math.sqrt(B_HEAD_DIM)),
                         jnp.tile(qk_gain_k, b_heads)])[:, None, :]

    x2d = x.reshape(m, d)
    proj, gates = _in_proj(x2d, norm_gain[None, :], w_main, w_gate, conv_w, conv_b[None, :], qk_gain,
                           seq=seq, tm=min(1024, seq))

    g8 = gates[:, :2 * A_HEADS].reshape(batch, nc, A_CHUNK, 2 * A_HEADS).transpose(0, 3, 1, 2)
    grow, col = _mlstm_gates(jnp.stack([i_bias, f_bias]), g8, nc=nc)
    ha = _mlstm_chunks(proj, grow, col, a_head_gain.reshape(A_HEADS, 1, hd_a), batch=batch, seq=seq, d=d)

    ob = _stickbreak(proj, batch=batch, seq=seq, d=d, tq=512)

    out = _out_proj(x2d, ha, ob, proj, w_branch_a.astype(BF16), w_branch_b.astype(BF16),
                    w_out.astype(BF16), tm=512)
    return out.reshape(batch, seq, d)


def kernel(x, norm_gain, w_in, conv_w, conv_b, i_bias, f_bias, a_head_gain, qk_gain_q, qk_gain_k,
           w_branch_a, w_branch_b, w_out):
    for layer in range(norm_gain.shape[0]):
        x = _layer(x, norm_gain[layer], w_in[layer], conv_w[layer], conv_b[layer], i_bias[layer],
                   f_bias[layer], a_head_gain[layer], qk_gain_q[layer], qk_gain_k[layer],
                   w_branch_a[layer], w_branch_b[layer], w_out[layer])
    return x
```

```python
import functools
import math

import jax
import jax.numpy as jnp
from jax import lax
from jax.experimental import pallas as pl
from jax.experimental.pallas import tpu as pltpu

F32 = jnp.float32
BF16 = jnp.bfloat16

NORM_EPS = 1e-6
A_HEADS = 4
A_CHUNK = 128
CONV_WIDTH = 4
B_HEAD_DIM = 128
LANES = 128
SUBLANES = 8
VMEM_LIMIT = 56 * 1024 * 1024

T_QA, T_KA, T_VA, T_OA, T_ZA, T_QB, T_KB, T_VB, T_ZB, T_GA, T_GB = range(11)
N_TILES = 11
GATE_SLOTS = 16
LOG2_E = 1.0 / math.log(2.0)
STICK_EXHAUSTED_LOG2 = 160.0


def _sigmoid(v):
    return 0.5 * (jnp.tanh(0.5 * v) + 1.0)


def _in_proj_body(x_ref, ng_ref, w_ref, wg_ref, cw_ref, cb_ref, qkg_ref, proj_ref, gates_ref,
                  hn_ref, halo_ref, *, tm, d, tiles_per_seq, rc, k_scale):
    i = pl.program_id(0)
    j = pl.program_id(1)

    @pl.when(j == 0)
    def _():
        def norm_rows(r, carry):
            rows = pl.ds(pl.multiple_of(r * rc, rc), rc)
            xf = x_ref[rows, :]
            ms = jnp.mean(xf * xf, axis=-1, keepdims=True)
            hn_ref[rows, :] = (xf * lax.rsqrt(ms + NORM_EPS) * ng_ref[...]).astype(BF16)
            return carry
        lax.fori_loop(0, tm // rc, norm_rows, 0)
        gates_ref[...] = jnp.dot(hn_ref[...], wg_ref[...], preferred_element_type=F32)

    def project():
        return jnp.dot(hn_ref[...], w_ref[...], preferred_element_type=F32)

    is_conv = j <= T_KA
    is_qkn = jnp.logical_or(j == T_QB, j == T_KB)

    @pl.when(is_conv)
    def _():
        acc = project()
        first = (i % tiles_per_seq) == 0
        window = jnp.concatenate([jnp.where(first, 0.0, halo_ref[j]), acc], axis=0)
        halo_ref[j] = acc[tm - SUBLANES:, :]
        y = cw_ref[0:1, :] * window
        for tap in range(1, CONV_WIDTH):
            y = cw_ref[tap:tap + 1, :] * window + pltpu.roll(y, 1, 0)
        y = y[SUBLANES:, :] + cb_ref[...]
        half_post = jnp.where(j == T_KA, 0.5 * k_scale, 0.5)
        proj_ref[...] = (y * half_post * (jnp.tanh(0.5 * y) + 1.0)).astype(BF16)

    @pl.when(is_qkn)
    def _():
        acc = project()
        for hh in range(d // B_HEAD_DIM):
            cols = slice(hh * B_HEAD_DIM, (hh + 1) * B_HEAD_DIM)
            blk = acc[:, cols]
            ms = jnp.mean(blk * blk, axis=-1, keepdims=True)
            proj_ref[:, cols] = (blk * lax.rsqrt(ms + NORM_EPS) * qkg_ref[0, :, cols]).astype(BF16)

    @pl.when(jnp.logical_not(jnp.logical_or(is_conv, is_qkn)))
    def _():
        proj_ref[...] = project().astype(BF16)


def _in_proj(x2d, norm_gain, w_main, w_gate, conv_w, conv_b, qk_gain, *, seq, tm):
    m, d = x2d.shape
    body = functools.partial(_in_proj_body, tm=tm, d=d, tiles_per_seq=seq // tm, rc=128,
                             k_scale=1.0 / math.sqrt(d // A_HEADS))
    return pl.pallas_call(
        body,
        grid=(m // tm, N_TILES),
        in_specs=[
            pl.BlockSpec((tm, d), lambda i, j: (i, 0)),
            pl.BlockSpec((1, d), lambda i, j: (0, 0)),
            pl.BlockSpec((d, d), lambda i, j: (0, j)),
            pl.BlockSpec((d, LANES), lambda i, j: (0, 0)),
            pl.BlockSpec((CONV_WIDTH, d), lambda i, j: (0, jnp.minimum(j, T_KA))),
            pl.BlockSpec((1, d), lambda i, j: (0, jnp.minimum(j, T_KA))),
            pl.BlockSpec((1, 1, d), lambda i, j: (jnp.clip(j - T_QB, 0, 1), 0, 0)),
        ],
        out_specs=[
            pl.BlockSpec((tm, d), lambda i, j: (i, j)),
            pl.BlockSpec((tm, LANES), lambda i, j: (i, 0)),
        ],
        out_shape=[
            jax.ShapeDtypeStruct((m, N_TILES * d), BF16),
            jax.ShapeDtypeStruct((m, LANES), F32),
        ],
        scratch_shapes=[
            pltpu.VMEM((tm, d), BF16),
            pltpu.VMEM((2, SUBLANES, d), F32),
        ],
        compiler_params=pltpu.CompilerParams(
            dimension_semantics=("arbitrary", "arbitrary"), vmem_limit_bytes=VMEM_LIMIT),
        name="in_proj",
    )(x2d, norm_gain, w_main, w_gate, conv_w, conv_b, qk_gain)


def _lane_scan(v, op, ident):
    lane = lax.broadcasted_iota(jnp.int32, v.shape, 1)
    shift = 1
    while shift < v.shape[1]:
        moved = pltpu.roll(v, shift, 1)
        v = op(v, jnp.where(lane >= shift, moved, ident))
        shift *= 2
    return v


def _gates_body(bias_ref, ia_ref, fa_ref, grow_ref, col_ref, mprev_ref, *, nc):
    h = pl.program_id(1)
    L = A_CHUNK
    i_pre = ia_ref[0, 0] + bias_ref[0, h]
    f_pre = fa_ref[0, 0] + bias_ref[1, h]
    logf = jnp.minimum(f_pre, 0.0) - jnp.log1p(jnp.exp(-jnp.abs(f_pre)))
    bcum = _lane_scan(logf, jnp.add, 0.0)
    b_last = jnp.broadcast_to(bcum[:, L - 1:L], (nc, L))
    a = b_last - bcum + i_pre
    a_max = jnp.broadcast_to(jnp.max(a, axis=-1, keepdims=True), (nc, L))
    wa = jnp.exp(a - a_max)
    g = i_pre - bcum
    g_cmax = _lane_scan(g, jnp.maximum, -jnp.inf)

    m_run = jnp.zeros((1, L), F32)
    for c in range(nc):
        mprev_ref[c:c + 1, :] = m_run
        m_run = jnp.maximum(b_last[c:c + 1, :] + m_run, a_max[c:c + 1, :])
    m_prev = mprev_ref[...]

    m_rel = jnp.maximum(m_prev, g_cmax)
    inter_w = jnp.exp(m_prev - m_rel)
    exp_neg_m = jnp.exp(-(bcum + m_rel))
    m_new = jnp.maximum(b_last + m_prev, a_max)
    decay = jnp.exp(b_last + m_prev - m_new)
    scale = jnp.exp(a_max - m_new)

    grow_ref[0, 0] = g
    zeros = jnp.zeros((GATE_SLOTS, L), F32)
    for blk in range(nc // GATE_SLOTS):
        rows = slice(blk * GATE_SLOTS, (blk + 1) * GATE_SLOTS)
        packed = jnp.concatenate(
            [m_rel[rows], inter_w[rows], exp_neg_m[rows], wa[rows], decay[rows], scale[rows], zeros, zeros],
            axis=0)
        col_ref[0, 0, blk] = packed.T


def _mlstm_gates(bias, g8, *, nc):
    b = g8.shape[0]
    L = A_CHUNK
    return pl.pallas_call(
        functools.partial(_gates_body, nc=nc),
        grid=(b, A_HEADS),
        in_specs=[
            pl.BlockSpec(memory_space=pltpu.SMEM),
            pl.BlockSpec((1, 1, nc, L), lambda bi, h: (bi, h, 0, 0)),
            pl.BlockSpec((1, 1, nc, L), lambda bi, h: (bi, A_HEADS + h, 0, 0)),
        ],
        out_specs=[
            pl.BlockSpec((1, 1, nc, L), lambda bi, h: (bi, h, 0, 0)),
            pl.BlockSpec((1, 1, nc // GATE_SLOTS, L, LANES), lambda bi, h: (bi, h, 0, 0, 0)),
        ],
        out_shape=[
            jax.ShapeDtypeStruct((b, A_HEADS, nc, L), F32),
            jax.ShapeDtypeStruct((b, A_HEADS, nc // GATE_SLOTS, L, LANES), F32),
        ],
        scratch_shapes=[pltpu.VMEM((nc, L), F32)],
        compiler_params=pltpu.CompilerParams(dimension_semantics=("parallel", "parallel")),
        name="mlstm_gates",
    )(bias, g8, g8)


def _mlstm_body(q_ref, k_ref, v_ref, o_ref, z_ref, grow_ref, col_ref, gain_ref, out_ref, state_ref, *, hd):
    L = A_CHUNK
    cb = pl.program_id(2)

    @pl.when(cb == 0)
    def _():
        state_ref[...] = jnp.zeros_like(state_ref)

    col = col_ref[0, 0, 0]
    row_i = lax.broadcasted_iota(jnp.int32, (L, L), 0)
    col_i = lax.broadcasted_iota(jnp.int32, (L, L), 1)
    causal = col_i <= row_i
    ones_b = jnp.ones((L, LANES), BF16)
    gain = gain_ref[0]

    for c in range(GATE_SLOTS):
        rows = slice(c * L, (c + 1) * L)

        def colvec(quantity):
            lane = quantity * GATE_SLOTS + c
            return col[:, lane:lane + 1]

        m_rel, inter_w, exp_neg_m, wa = colvec(0), colvec(1), colvec(2), colvec(3)
        decay = colvec(4)[0:1, :]
        scale = colvec(5)[0:1, :]
        g_row = grow_ref[0, 0, c:c + 1, :]

        q = q_ref[rows, :]
        k = k_ref[rows, :]
        v_aug = jnp.concatenate([v_ref[rows, :], ones_b], axis=1)

        dmat = jnp.exp(jnp.where(causal, g_row - m_rel, -jnp.inf))
        scores = lax.dot_general(q, k, (((1,), (1,)), ((), ())), preferred_element_type=F32) * dmat
        state = state_ref[...]
        total = (jnp.dot(scores.astype(BF16), v_aug, preferred_element_type=F32)
                 + inter_w * jnp.dot(q, state.astype(BF16), preferred_element_type=F32))
        num = total[:, :hd]
        den = total[:, hd:]
        inv = 1.0 / jnp.maximum(jnp.abs(den), exp_neg_m)
        h_tilde = num * jnp.concatenate([inv] * (hd // LANES), axis=1)

        gated = _sigmoid(o_ref[rows, :].astype(F32)) * h_tilde
        ms = jnp.mean(gated * gated, axis=-1, keepdims=True)
        h_a = gated * lax.rsqrt(ms + NORM_EPS) * gain
        z = z_ref[rows, :].astype(F32)
        out_ref[rows, :] = (h_a * (z * _sigmoid(z))).astype(BF16)

        kw = (k.astype(F32) * wa).astype(BF16)
        update = lax.dot_general(kw, v_aug, (((0,), (0,)), ((), ())), preferred_element_type=F32)
        state_ref[...] = decay * state + scale * update


def _mlstm_chunks(proj, grow, col, head_gain, *, batch, seq, d):
    hd = d // A_HEADS
    L = A_CHUNK
    rows = GATE_SLOTS * L
    blocks_per_seq = seq // rows
    heads_per_tile = d // hd

    def col_block(tile):
        return pl.BlockSpec((rows, hd), lambda b, h, cb: (b * blocks_per_seq + cb, tile * heads_per_tile + h))

    return pl.pallas_call(
        functools.partial(_mlstm_body, hd=hd),
        grid=(batch, A_HEADS, blocks_per_seq),
        in_specs=[
            col_block(T_QA), col_block(T_KA), col_block(T_VA), col_block(T_OA), col_block(T_ZA),
            pl.BlockSpec((1, 1, GATE_SLOTS, L), lambda b, h, cb: (b, h, cb, 0)),
            pl.BlockSpec((1, 1, 1, L, LANES), lambda b, h, cb: (b, h, cb, 0, 0)),
            pl.BlockSpec((1, 1, hd), lambda b, h, cb: (h, 0, 0)),
        ],
        out_specs=pl.BlockSpec((rows, hd), lambda b, h, cb: (b * blocks_per_seq + cb, h)),
        out_shape=jax.ShapeDtypeStruct((batch * seq, d), BF16),
        scratch_shapes=[pltpu.VMEM((hd, hd + LANES), F32)],
        compiler_params=pltpu.CompilerParams(
            dimension_semantics=("parallel", "parallel", "arbitrary"), vmem_limit_bytes=VMEM_LIMIT),
        name="mlstm_chunks",
    )(proj, proj, proj, proj, proj, grow, col, head_gain)


def _stickbreak_body(q_ref, k_ref, v_ref, out_ref, *, tq, tk, heads_per_step):
    qi = pl.program_id(2)
    hd = B_HEAD_DIM
    q0 = pl.multiple_of(qi * tq, tq)
    n_sub = tq // tk

    r_i = lax.broadcasted_iota(jnp.int32, (2 * tk, 2 * tk), 0) % tk
    c_i = lax.broadcasted_iota(jnp.int32, (2 * tk, 2 * tk), 1)
    suffix_op = jnp.where(jnp.logical_or(c_i >= tk, r_i > c_i), 1.0, 0.0).astype(BF16)
    strictly_before = (lax.broadcasted_iota(jnp.int32, (tq, tq), 1)
                       < lax.broadcasted_iota(jnp.int32, (tq, tq), 0))

    def span(q, head, k0, carry, *, diagonal=False, gate=None):
        consumed, acc = carry
        cols_h = slice(head * hd, (head + 1) * hd)
        kblk = k_ref[pl.ds(k0, tq), cols_h]
        vblk = v_ref[pl.ds(k0, tq), cols_h]
        z2 = lax.dot_general(q, kblk, (((1,), (1,)), ((), ())), preferred_element_type=F32)
        sp2 = jnp.maximum(z2, 0.0) + jnp.log(1.0 + jnp.exp2(-jnp.abs(z2))) * LOG2_E
        if diagonal:
            sp2 = jnp.where(strictly_before, sp2, 0.0)
        hi = sp2.astype(BF16)
        lo = (sp2 - hi.astype(F32)).astype(BF16)
        log2_beta = z2 - sp2
        w_parts = [None] * n_sub
        for blk in reversed(range(n_sub)):
            cols = slice(blk * tk, (blk + 1) * tk)
            sums = jnp.dot(jnp.concatenate([hi[:, cols], lo[:, cols]], axis=1), suffix_op,
                           preferred_element_type=F32)
            w_parts[blk] = jnp.exp2(log2_beta[:, cols] - (sums[:, :tk] + consumed))
            consumed = consumed + (sums[:, tk:] if gate is None else sums[:, tk:] * gate)
        w = jnp.concatenate(w_parts, axis=1)
        if diagonal:
            w = jnp.where(strictly_before, w, 0.0)
        pv = jnp.dot(w.astype(BF16), vblk, preferred_element_type=F32)
        return consumed, acc + (pv if gate is None else pv * gate)

    has_prev = jnp.where(qi > 0, 1.0, 0.0)
    prev0 = pl.multiple_of(jnp.maximum(q0 - tq, 0), tq)
    qs, carries = [], []
    for head in range(heads_per_step):
        q = q_ref[:, head * hd:(head + 1) * hd]
        carry = (jnp.zeros((tq, tk), F32), jnp.zeros((tq, hd), F32))
        carry = span(q, head, q0, carry, diagonal=True)
        carry = span(q, head, prev0, carry, gate=has_prev)
        qs.append(q)
        carries.append(carry)

    for head in range(heads_per_step):
        def more(state):
            it, consumed, _ = state
            return jnp.logical_and(it < qi - 1, jnp.min(consumed) < STICK_EXHAUSTED_LOG2)

        def step(state, head=head):
            it, consumed, acc = state
            k0 = pl.multiple_of((qi - 2 - it) * tq, tq)
            consumed, acc = span(qs[head], head, k0, (consumed, acc))
            return it + 1, consumed, acc

        _, _, acc = lax.while_loop(more, step, (jnp.int32(0),) + carries[head])
        out_ref[:, head * hd:(head + 1) * hd] = acc.astype(BF16)


def _stickbreak(proj, *, batch, seq, d, tq, heads_per_step):
    hd = B_HEAD_DIM
    groups = d // (hd * heads_per_step)
    width = hd * heads_per_step
    nq = seq // tq
    return pl.pallas_call(
        functools.partial(_stickbreak_body, tq=tq, tk=LANES, heads_per_step=heads_per_step),
        grid=(batch, groups, nq),
        in_specs=[
            pl.BlockSpec((tq, width), lambda b, g, qi: (b * nq + qi, T_QB * groups + g)),
            pl.BlockSpec((seq, width), lambda b, g, qi: (b, T_KB * groups + g)),
            pl.BlockSpec((seq, width), lambda b, g, qi: (b, T_VB * groups + g)),
        ],
        out_specs=pl.BlockSpec((tq, width), lambda b, g, qi: (b * nq + qi, g)),
        out_shape=jax.ShapeDtypeStruct((batch * seq, d), BF16),
        compiler_params=pltpu.CompilerParams(
            dimension_semantics=("parallel", "parallel", "arbitrary"), vmem_limit_bytes=VMEM_LIMIT),
        name="stickbreak",
    )(proj, proj, proj)


def _out_proj_body(x_ref, ha_ref, ob_ref, zb_ref, ga_ref, gb_ref, wa_ref, wb_ref, wo_ref, out_ref):
    zb = zb_ref[...].astype(F32)
    ob = (ob_ref[...].astype(F32) * (zb * _sigmoid(zb))).astype(BF16)
    y_a = jnp.dot(ha_ref[...], wa_ref[...], preferred_element_type=F32)
    y_b = jnp.dot(ob, wb_ref[...], preferred_element_type=F32)
    merged = _sigmoid(ga_ref[...].astype(F32)) * y_a + _sigmoid(gb_ref[...].astype(F32)) * y_b
    out_ref[...] = x_ref[...] + jnp.dot(merged.astype(BF16), wo_ref[...], preferred_element_type=F32)


def _out_proj(x2d, ha, ob, proj, w_a, w_b, w_o, *, tm):
    m, d = x2d.shape
    row_tile = pl.BlockSpec((tm, d), lambda i: (i, 0))
    weight = pl.BlockSpec((d, d), lambda i: (0, 0))

    def proj_tile(tile):
        return pl.BlockSpec((tm, d), lambda i: (i, tile))

    return pl.pallas_call(
        _out_proj_body,
        grid=(m // tm,),
        in_specs=[row_tile, row_tile, row_tile, proj_tile(T_ZB), proj_tile(T_GA), proj_tile(T_GB),
                  weight, weight, weight],
        out_specs=row_tile,
        out_shape=jax.ShapeDtypeStruct((m, d), F32),
        compiler_params=pltpu.CompilerParams(
            dimension_semantics=("parallel",), vmem_limit_bytes=VMEM_LIMIT),
        name="out_proj",
    )(x2d, ha, ob, proj, proj, proj, w_a, w_b, w_o)


def _layer(x, norm_gain, w_in, conv_w, conv_b, i_bias, f_bias, a_head_gain, qk_gain_q, qk_gain_k,
           w_branch_a, w_branch_b, w_out):
    batch, seq, d = x.shape
    m = batch * seq
    nc = seq // A_CHUNK
    a_w = d
    hd_a = d // A_HEADS
    b_heads = d // B_HEAD_DIM
    assert seq % (GATE_SLOTS * A_CHUNK) == 0 and d % LANES == 0

    gate_lo = 5 * a_w
    gate_hi = gate_lo + 2 * A_HEADS
    w_main = jnp.concatenate([w_in[:, :gate_lo], w_in[:, gate_hi:]], axis=1).astype(BF16)
    w_gate = jnp.pad(w_in[:, gate_lo:gate_hi], ((0, 0), (0, LANES - 2 * A_HEADS))).astype(BF16)
    qk_gain = jnp.stack([jnp.tile(qk_gain_q, b_heads) * (LOG2_E / math.sqrt(B_HEAD_DIM)),
                         jnp.tile(qk_gain_k, b_heads)])[:, None, :]

    x2d = x.reshape(m, d)
    proj, gates = _in_proj(x2d, norm_gain[None, :], w_main, w_gate, conv_w, conv_b[None, :], qk_gain,
                           seq=seq, tm=min(1024, seq))

    g8 = gates[:, :2 * A_HEADS].reshape(batch, nc, A_CHUNK, 2 * A_HEADS).transpose(0, 3, 1, 2)
    grow, col = _mlstm_gates(jnp.stack([i_bias, f_bias]), g8, nc=nc)
    ha = _mlstm_chunks(proj, grow, col, a_head_gain.reshape(A_HEADS, 1, hd_a), batch=batch, seq=seq, d=d)

    ob = _stickbreak(proj, batch=batch, seq=seq, d=d, tq=256, heads_per_step=4)

    out = _out_proj(x2d, ha, ob, proj, w_branch_a.astype(BF16), w_branch_b.astype(BF16),
                    w_out.astype(BF16), tm=512)
    return out.reshape(batch, seq, d)


def kernel(x, norm_gain, w_in, conv_w, conv_b, i_bias, f_bias, a_head_gain, qk_gain_q, qk_gain_k,
           w_branch_a, w_branch_b, w_out):
    for layer in range(norm_gain.shape[0]):
        x = _layer(x, norm_gain[layer], w_in[layer], conv_w[layer], conv_b[layer], i_bias[layer],
                   f_bias[layer], a_head_gain[layer], qk_gain_q[layer], qk_gain_k[layer],
                   w_branch_a[layer], w_branch_b[layer], w_out[layer])
    return x
```

```python
import functools
import math

import jax
import jax.numpy as jnp
from jax import lax
from jax.experimental import pallas as pl
from jax.experimental.pallas import tpu as pltpu

F32 = jnp.float32
BF16 = jnp.bfloat16

NORM_EPS = 1e-6
A_HEADS = 4
A_CHUNK = 128
CONV_WIDTH = 4
B_HEAD_DIM = 128
LANES = 128
SUBLANES = 8
VMEM_LIMIT = 56 * 1024 * 1024

T_QA, T_KA, T_VA, T_OA, T_ZA, T_QB, T_KB, T_VB, T_ZB, T_GA, T_GB = range(11)
N_TILES = 11
N_CONV = 2
HALVED_TILES = (T_OA, T_ZA, T_ZB, T_GA, T_GB)
GATE_SLOTS = 16
LOG2_E = 1.0 / math.log(2.0)
STICK_EXHAUSTED_LOG2 = 160.0


def _silu_of_twice(u):
    return u * (jnp.tanh(u) + 1.0)


def _in_proj_body(x_ref, ng_ref, w_ref, wg_ref, cw_ref, cb_ref, qkg_ref, proj_ref, qk_ref, gates_ref,
                  hn_ref, halo_ref, *, tm, d, tiles_per_seq, rc, k_scale):
    i = pl.program_id(0)
    j = pl.program_id(1)
    first = (i % tiles_per_seq) == 0

    def project():
        return jnp.dot(hn_ref[...], w_ref[...], preferred_element_type=F32)

    def conv_silu(slot):
        acc = project()
        window = jnp.concatenate([jnp.where(first, 0.0, halo_ref[slot]), acc], axis=0)
        halo_ref[slot] = acc[tm - SUBLANES:, :]
        u = cw_ref[slot, 0:1, :] * window
        for tap in range(1, CONV_WIDTH):
            u = cw_ref[slot, tap:tap + 1, :] * window + pltpu.roll(u, 1, 0)
        y = _silu_of_twice(u[SUBLANES:, :] + cb_ref[slot])
        qk_ref[...] = (y * k_scale if slot == T_KA else y).astype(BF16)

    @pl.when(j == T_QA)
    def _():
        def norm_rows(r, carry):
            rows = pl.ds(pl.multiple_of(r * rc, rc), rc)
            xf = x_ref[rows, :]
            ms = jnp.mean(xf * xf, axis=-1, keepdims=True)
            hn_ref[rows, :] = (xf * lax.rsqrt(ms + NORM_EPS) * ng_ref[...]).astype(BF16)
            return carry
        lax.fori_loop(0, tm // rc, norm_rows, 0)
        gates_ref[...] = jnp.dot(hn_ref[...], wg_ref[...], preferred_element_type=F32)
        conv_silu(T_QA)

    @pl.when(j == T_KA)
    def _():
        conv_silu(T_KA)

    @pl.when(jnp.logical_or(j == T_QB, j == T_KB))
    def _():
        acc = project()
        for hh in range(d // B_HEAD_DIM):
            cols = slice(hh * B_HEAD_DIM, (hh + 1) * B_HEAD_DIM)
            blk = acc[:, cols]
            ms = jnp.mean(blk * blk, axis=-1, keepdims=True)
            proj_ref[:, cols] = (blk * lax.rsqrt(ms + NORM_EPS) * qkg_ref[0, :, cols]).astype(BF16)

    is_plain = functools.reduce(jnp.logical_or, [j == t for t in (T_VA, T_OA, T_ZA, T_VB, T_ZB, T_GA, T_GB)])

    @pl.when(is_plain)
    def _():
        proj_ref[...] = project().astype(BF16)


def _in_proj(x2d, norm_gain, w_main, w_gate, conv_w, conv_b, qk_gain, *, seq, tm):
    m, d = x2d.shape
    body = functools.partial(_in_proj_body, tm=tm, d=d, tiles_per_seq=seq // tm, rc=128,
                             k_scale=1.0 / math.sqrt(d // A_HEADS))
    return pl.pallas_call(
        body,
        grid=(m // tm, N_TILES),
        in_specs=[
            pl.BlockSpec((tm, d), lambda i, j: (i, 0), pipeline_mode=pl.Buffered(1)),
            pl.BlockSpec((1, d), lambda i, j: (0, 0)),
            pl.BlockSpec((d, d), lambda i, j: (0, j)),
            pl.BlockSpec((d, LANES), lambda i, j: (0, 0)),
            pl.BlockSpec((N_CONV, CONV_WIDTH, d), lambda i, j: (0, 0, 0)),
            pl.BlockSpec((N_CONV, 1, d), lambda i, j: (0, 0, 0)),
            pl.BlockSpec((1, 1, d), lambda i, j: (jnp.clip(j - T_QB, 0, 1), 0, 0)),
        ],
        out_specs=[
            pl.BlockSpec((tm, d), lambda i, j: (i, jnp.maximum(j - N_CONV, 0))),
            pl.BlockSpec((tm, d), lambda i, j: (i, jnp.minimum(j, N_CONV - 1))),
            pl.BlockSpec((tm, LANES), lambda i, j: (i, 0)),
        ],
        out_shape=[
            jax.ShapeDtypeStruct((m, (N_TILES - N_CONV) * d), BF16),
            jax.ShapeDtypeStruct((m, N_CONV * d), BF16),
            jax.ShapeDtypeStruct((m, LANES), F32),
        ],
        scratch_shapes=[
            pltpu.VMEM((tm, d), BF16),
            pltpu.VMEM((N_CONV, SUBLANES, d), F32),
        ],
        compiler_params=pltpu.CompilerParams(
            dimension_semantics=("arbitrary", "arbitrary"), vmem_limit_bytes=VMEM_LIMIT),
        name="in_proj",
    )(x2d, norm_gain, w_main, w_gate, conv_w, conv_b, qk_gain)


def _lane_scan(v, op, ident):
    lane = lax.broadcasted_iota(jnp.int32, v.shape, 1)
    shift = 1
    while shift < v.shape[1]:
        moved = pltpu.roll(v, shift, 1)
        v = op(v, jnp.where(lane >= shift, moved, ident))
        shift *= 2
    return v


def _gates_body(bias_ref, ia_ref, fa_ref, grow_ref, col_ref, mprev_ref, *, nc):
    h = pl.program_id(1)
    L = A_CHUNK
    i_pre = ia_ref[0] + bias_ref[0, h]
    f_pre = fa_ref[0] + bias_ref[1, h]
    logf = jnp.minimum(f_pre, 0.0) - jnp.log1p(jnp.exp(-jnp.abs(f_pre)))
    bcum = _lane_scan(logf, jnp.add, 0.0)
    b_last = jnp.broadcast_to(bcum[:, L - 1:L], (nc, L))
    a = b_last - bcum + i_pre
    a_max = jnp.broadcast_to(jnp.max(a, axis=-1, keepdims=True), (nc, L))
    wa = jnp.exp(a - a_max)
    g = i_pre - bcum
    g_cmax = _lane_scan(g, jnp.maximum, -jnp.inf)

    m_run = jnp.zeros((1, L), F32)
    for c in range(nc):
        mprev_ref[c:c + 1, :] = m_run
        m_run = jnp.maximum(b_last[c:c + 1, :] + m_run, a_max[c:c + 1, :])
    m_prev = mprev_ref[...]

    m_rel = jnp.maximum(m_prev, g_cmax)
    inter_w = jnp.exp(m_prev - m_rel)
    exp_neg_m = jnp.exp(-(bcum + m_rel))
    m_new = jnp.maximum(b_last + m_prev, a_max)
    decay = jnp.exp(b_last + m_prev - m_new)
    wa_scaled = wa * jnp.exp(a_max - m_new)

    grow_ref[0, 0] = g
    zeros = jnp.zeros((GATE_SLOTS, L), F32)
    for blk in range(nc // GATE_SLOTS):
        rows = slice(blk * GATE_SLOTS, (blk + 1) * GATE_SLOTS)
        packed = jnp.concatenate(
            [m_rel[rows], inter_w[rows], exp_neg_m[rows], wa_scaled[rows], decay[rows], zeros, zeros, zeros],
            axis=0)
        col_ref[0, 0, blk] = packed.T


def _mlstm_gates(bias, gates_t, *, batch, nc):
    L = A_CHUNK
    return pl.pallas_call(
        functools.partial(_gates_body, nc=nc),
        grid=(batch, A_HEADS),
        in_specs=[
            pl.BlockSpec(memory_space=pltpu.SMEM),
            pl.BlockSpec((1, nc, L), lambda bi, h: (h, bi, 0)),
            pl.BlockSpec((1, nc, L), lambda bi, h: (A_HEADS + h, bi, 0)),
        ],
        out_specs=[
            pl.BlockSpec((1, 1, nc, L), lambda bi, h: (bi, h, 0, 0)),
            pl.BlockSpec((1, 1, nc // GATE_SLOTS, L, LANES), lambda bi, h: (bi, h, 0, 0, 0)),
        ],
        out_shape=[
            jax.ShapeDtypeStruct((batch, A_HEADS, nc, L), F32),
            jax.ShapeDtypeStruct((batch, A_HEADS, nc // GATE_SLOTS, L, LANES), F32),
        ],
        scratch_shapes=[pltpu.VMEM((nc, L), F32)],
        compiler_params=pltpu.CompilerParams(dimension_semantics=("parallel", "parallel")),
        name="mlstm_gates",
    )(bias, gates_t, gates_t)


def _mlstm_body(q_ref, k_ref, v_ref, o_ref, z_ref, grow_ref, col_ref, gain_ref, out_ref, state_ref, *, hd):
    L = A_CHUNK
    cb = pl.program_id(2)

    @pl.when(cb == 0)
    def _():
        state_ref[...] = jnp.zeros_like(state_ref)

    col = col_ref[0, 0, 0]
    row_i = lax.broadcasted_iota(jnp.int32, (L, L), 0)
    col_i = lax.broadcasted_iota(jnp.int32, (L, L), 1)
    causal = col_i <= row_i
    ones_b = jnp.ones((L, LANES), BF16)
    gain = gain_ref[0]

    for c in range(GATE_SLOTS):
        rows = slice(c * L, (c + 1) * L)

        def colvec(quantity):
            lane = quantity * GATE_SLOTS + c
            return col[:, lane:lane + 1]

        m_rel, inter_w, exp_neg_m, wa_scaled = colvec(0), colvec(1), colvec(2), colvec(3)
        decay = colvec(4)[0:1, :]
        g_row = grow_ref[0, 0, c:c + 1, :]

        q = q_ref[rows, :]
        k = k_ref[rows, :]
        v_aug = jnp.concatenate([v_ref[rows, :], ones_b], axis=1)

        dmat = jnp.exp(jnp.where(causal, g_row - m_rel, -jnp.inf))
        scores = lax.dot_general(q, k, (((1,), (1,)), ((), ())), preferred_element_type=F32) * dmat
        state = state_ref[...]
        total = (jnp.dot(scores.astype(BF16), v_aug, preferred_element_type=F32)
                 + inter_w * jnp.dot(q, state.astype(BF16), preferred_element_type=F32))
        num = total[:, :hd]
        den = total[:, hd:]
        inv = 1.0 / jnp.maximum(jnp.abs(den), exp_neg_m)
        h_tilde = num * jnp.concatenate([inv] * (hd // LANES), axis=1)

        gated2 = (jnp.tanh(o_ref[rows, :].astype(F32)) + 1.0) * h_tilde
        ms2 = jnp.mean(gated2 * gated2, axis=-1, keepdims=True)
        h_a = gated2 * lax.rsqrt(ms2 + 4.0 * NORM_EPS) * gain
        out_ref[rows, :] = (h_a * _silu_of_twice(z_ref[rows, :].astype(F32))).astype(BF16)

        kw = (k.astype(F32) * wa_scaled).astype(BF16)
        update = lax.dot_general(kw, v_aug, (((0,), (0,)), ((), ())), preferred_element_type=F32)
        state_ref[...] = decay * state + update


def _mlstm_chunks(qk, proj, grow, col, head_gain, *, batch, seq, d):
    hd = d // A_HEADS
    L = A_CHUNK
    rows = GATE_SLOTS * L
    blocks_per_seq = seq // rows
    heads_per_tile = d // hd

    def col_block(tile):
        return pl.BlockSpec((rows, hd), lambda b, h, cb: (b * blocks_per_seq + cb, tile * heads_per_tile + h))

    return pl.pallas_call(
        functools.partial(_mlstm_body, hd=hd),
        grid=(batch, A_HEADS, blocks_per_seq),
        in_specs=[
            col_block(T_QA), col_block(T_KA),
            col_block(T_VA - N_CONV), col_block(T_OA - N_CONV), col_block(T_ZA - N_CONV),
            pl.BlockSpec((1, 1, GATE_SLOTS, L), lambda b, h, cb: (b, h, cb, 0)),
            pl.BlockSpec((1, 1, 1, L, LANES), lambda b, h, cb: (b, h, cb, 0, 0)),
            pl.BlockSpec((1, 1, hd), lambda b, h, cb: (h, 0, 0)),
        ],
        out_specs=pl.BlockSpec((rows, hd), lambda b, h, cb: (b * blocks_per_seq + cb, h)),
        out_shape=jax.ShapeDtypeStruct((batch * seq, d), BF16),
        scratch_shapes=[pltpu.VMEM((hd, hd + LANES), F32)],
        compiler_params=pltpu.CompilerParams(
            dimension_semantics=("parallel", "parallel", "arbitrary"), vmem_limit_bytes=VMEM_LIMIT),
        name="mlstm_chunks",
    )(qk, qk, proj, proj, proj, grow, col, head_gain)


def _stickbreak_body(q_ref, k_ref, v_ref, out_ref, *, tq, heads_per_step):
    qi = pl.program_id(2)
    hd = B_HEAD_DIM
    q0 = pl.multiple_of(qi * tq, tq)

    row_i = lax.broadcasted_iota(jnp.int32, (tq, tq), 0)
    col_i = lax.broadcasted_iota(jnp.int32, (tq, tq), 1)
    strictly_before = col_i < row_i
    suffix_op = jnp.where(row_i > col_i, 1.0, 0.0).astype(BF16)

    def span(q, head, k0, carry, *, diagonal=False, gate=None):
        consumed, acc = carry
        cols_h = slice(head * hd, (head + 1) * hd)
        kblk = k_ref[pl.ds(k0, tq), cols_h]
        vblk = v_ref[pl.ds(k0, tq), cols_h]
        z2 = lax.dot_general(q, kblk, (((1,), (1,)), ((), ())), preferred_element_type=F32)
        sp2 = jnp.maximum(z2, 0.0) + jnp.log(1.0 + jnp.exp2(-jnp.abs(z2))) * LOG2_E
        if diagonal:
            sp2 = jnp.where(strictly_before, sp2, 0.0)
        later = jnp.dot(sp2.astype(BF16), suffix_op, preferred_element_type=F32) + consumed
        w = jnp.exp2(z2 - sp2 - later)
        if diagonal:
            w = jnp.where(strictly_before, w, 0.0)
        pv = jnp.dot(w.astype(BF16), vblk, preferred_element_type=F32)
        total = jnp.sum(sp2, axis=-1, keepdims=True)
        if gate is not None:
            total, pv = total * gate, pv * gate
        return consumed + total, acc + pv

    has_prev = jnp.where(qi > 0, 1.0, 0.0)
    prev0 = pl.multiple_of(jnp.maximum(q0 - tq, 0), tq)
    qs, carries = [], []
    for head in range(heads_per_step):
        q = q_ref[:, head * hd:(head + 1) * hd]
        carry = (jnp.zeros((tq, 1), F32), jnp.zeros((tq, hd), F32))
        carry = span(q, head, q0, carry, diagonal=True)
        carry = span(q, head, prev0, carry, gate=has_prev)
        qs.append(q)
        carries.append(carry)

    for head in range(heads_per_step):
        def more(state):
            it, consumed, _ = state
            return jnp.logical_and(it < qi - 1, jnp.min(consumed) < STICK_EXHAUSTED_LOG2)

        def step(state, head=head):
            it, consumed, acc = state
            k0 = pl.multiple_of((qi - 2 - it) * tq, tq)
            consumed, acc = span(qs[head], head, k0, (consumed, acc))
            return it + 1, consumed, acc

        _, _, acc = lax.while_loop(more, step, (jnp.int32(0),) + carries[head])
        out_ref[:, head * hd:(head + 1) * hd] = acc.astype(BF16)


def _stickbreak(proj, *, batch, seq, d, tq, heads_per_step):
    hd = B_HEAD_DIM
    groups = d // (hd * heads_per_step)
    width = hd * heads_per_step
    nq = seq // tq
    return pl.pallas_call(
        functools.partial(_stickbreak_body, tq=tq, heads_per_step=heads_per_step),
        grid=(batch, groups, nq),
        in_specs=[
            pl.BlockSpec((tq, width), lambda b, g, qi: (b * nq + qi, (T_QB - N_CONV) * groups + g)),
            pl.BlockSpec((seq, width), lambda b, g, qi: (b, (T_KB - N_CONV) * groups + g)),
            pl.BlockSpec((seq, width), lambda b, g, qi: (b, (T_VB - N_CONV) * groups + g)),
        ],
        out_specs=pl.BlockSpec((tq, width), lambda b, g, qi: (b * nq + qi, g)),
        out_shape=jax.ShapeDtypeStruct((batch * seq, d), BF16),
        compiler_params=pltpu.CompilerParams(
            dimension_semantics=("parallel", "parallel", "arbitrary"), vmem_limit_bytes=VMEM_LIMIT),
        name="stickbreak",
    )(proj, proj, proj)


def _out_proj_body(x_ref, ha_ref, ob_ref, zb_ref, ga_ref, gb_ref, wa_ref, wb_ref, wo_ref, out_ref):
    ob = (ob_ref[...].astype(F32) * _silu_of_twice(zb_ref[...].astype(F32))).astype(BF16)
    y_a = jnp.dot(ha_ref[...], wa_ref[...], preferred_element_type=F32)
    y_b = jnp.dot(ob, wb_ref[...], preferred_element_type=F32)
    merged2 = ((jnp.tanh(ga_ref[...].astype(F32)) + 1.0) * y_a
               + (jnp.tanh(gb_ref[...].astype(F32)) + 1.0) * y_b)
    out_ref[...] = x_ref[...] + jnp.dot(merged2.astype(BF16), wo_ref[...], preferred_element_type=F32)


def _out_proj(x2d, ha, ob, proj, w_a, w_b, w_o_half, *, tm):
    m, d = x2d.shape
    row_tile = pl.BlockSpec((tm, d), lambda i: (i, 0))
    weight = pl.BlockSpec((d, d), lambda i: (0, 0))

    def proj_tile(tile):
        return pl.BlockSpec((tm, d), lambda i: (i, tile - N_CONV))

    return pl.pallas_call(
        _out_proj_body,
        grid=(m // tm,),
        in_specs=[row_tile, row_tile, row_tile, proj_tile(T_ZB), proj_tile(T_GA), proj_tile(T_GB),
                  weight, weight, weight],
        out_specs=row_tile,
        out_shape=jax.ShapeDtypeStruct((m, d), F32),
        compiler_params=pltpu.CompilerParams(
            dimension_semantics=("parallel",), vmem_limit_bytes=VMEM_LIMIT),
        name="out_proj",
    )(x2d, ha, ob, proj, proj, proj, w_a, w_b, w_o_half)


def _layer(x, norm_gain, w_in, conv_w, conv_b, i_bias, f_bias, a_head_gain, qk_gain_q, qk_gain_k,
           w_branch_a, w_branch_b, w_out):
    batch, seq, d = x.shape
    m = batch * seq
    nc = seq // A_CHUNK
    hd_a = d // A_HEADS
    b_heads = d // B_HEAD_DIM
    assert seq % (GATE_SLOTS * A_CHUNK) == 0 and d % LANES == 0

    gate_lo = 5 * d
    gate_hi = gate_lo + 2 * A_HEADS
    tile_scale = jnp.ones((N_TILES,), F32).at[jnp.array(HALVED_TILES)].set(0.5)
    w_main = jnp.concatenate([w_in[:, :gate_lo], w_in[:, gate_hi:]], axis=1)
    w_main = (w_main * jnp.repeat(tile_scale, d)[None, :]).astype(BF16)
    w_gate = jnp.pad(w_in[:, gate_lo:gate_hi], ((0, 0), (0, LANES - 2 * A_HEADS))).astype(BF16)
    qk_gain = jnp.stack([jnp.tile(qk_gain_q, b_heads) * (LOG2_E / math.sqrt(B_HEAD_DIM)),
                         jnp.tile(qk_gain_k, b_heads)])[:, None, :]
    conv_w_half = 0.5 * conv_w.reshape(CONV_WIDTH, N_CONV, d).transpose(1, 0, 2)
    conv_b_half = 0.5 * conv_b.reshape(N_CONV, 1, d)

    x2d = x.reshape(m, d)
    proj, qk, gates = _in_proj(x2d, norm_gain[None, :], w_main, w_gate, conv_w_half, conv_b_half, qk_gain,
                               seq=seq, tm=min(2048, seq))

    gates_t = gates[:, :2 * A_HEADS].T.reshape(2 * A_HEADS, batch * nc, A_CHUNK)
    grow, col = _mlstm_gates(jnp.stack([i_bias, f_bias]), gates_t, batch=batch, nc=nc)
    ha = _mlstm_chunks(qk, proj, grow, col, a_head_gain.reshape(A_HEADS, 1, hd_a), batch=batch, seq=seq, d=d)

    ob = _stickbreak(proj, batch=batch, seq=seq, d=d, tq=256, heads_per_step=4)

    out = _out_proj(x2d, ha, ob, proj, w_branch_a.astype(BF16), w_branch_b.astype(BF16),
                    (0.5 * w_out).astype(BF16), tm=512)
    return out.reshape(batch, seq, d)


def kernel(x, norm_gain, w_in, conv_w, conv_b, i_bias, f_bias, a_head_gain, qk_gain_q, qk_gain_k,
           w_branch_a, w_branch_b, w_out):
    for layer in range(norm_gain.shape[0]):
        x = _layer(x, norm_gain[layer], w_in[layer], conv_w[layer], conv_b[layer], i_bias[layer],
                   f_bias[layer], a_head_gain[layer], qk_gain_q[layer], qk_gain_k[layer],
                   w_branch_a[layer], w_branch_b[layer], w_out[layer])
    return x
```

```python
import functools
import math

import jax
import jax.numpy as jnp
from jax import lax
from jax.experimental import pallas as pl
from jax.experimental.pallas import tpu as pltpu

F32 = jnp.float32
BF16 = jnp.bfloat16

NORM_EPS = 1e-6
A_HEADS = 4
A_CHUNK = 128
CONV_WIDTH = 4
B_HEAD_DIM = 128
LANES = 128
SUBLANES = 8
VMEM_LIMIT = 56 * 1024 * 1024

T_QA, T_KA, T_VA, T_OA, T_ZA, T_QB, T_KB, T_VB, T_ZB, T_GA, T_GB = range(11)
N_TILES = 11
N_CONV = 2
HALVED_TILES = (T_OA, T_ZA, T_ZB, T_GA, T_GB)
GATE_SLOTS = 16
LOG2_E = 1.0 / math.log(2.0)
STICK_EXHAUSTED_LOG2 = 160.0


def _silu_of_twice(u):
    return u * (jnp.tanh(u) + 1.0)


def _in_proj_body(x_ref, ng_ref, w_ref, wg_ref, cw_ref, cb_ref, qkg_ref, proj_ref, qk_ref, gates_ref,
                  hn_ref, halo_ref, *, tm, d, tiles_per_seq, rc, k_scale):
    i = pl.program_id(0)
    j = pl.program_id(1)
    first = (i % tiles_per_seq) == 0

    def project():
        return jnp.dot(hn_ref[...], w_ref[...], preferred_element_type=F32)

    def conv_silu(slot):
        acc = project()
        window = jnp.concatenate([jnp.where(first, 0.0, halo_ref[slot]), acc], axis=0)
        halo_ref[slot] = acc[tm - SUBLANES:, :]
        u = cw_ref[slot, 0:1, :] * window
        for tap in range(1, CONV_WIDTH):
            u = cw_ref[slot, tap:tap + 1, :] * window + pltpu.roll(u, 1, 0)
        y = _silu_of_twice(u[SUBLANES:, :] + cb_ref[slot])
        qk_ref[...] = (y * k_scale if slot == T_KA else y).astype(BF16)

    @pl.when(j == T_QA)
    def _():
        def norm_rows(r, carry):
            rows = pl.ds(pl.multiple_of(r * rc, rc), rc)
            xf = x_ref[rows, :]
            ms = jnp.mean(xf * xf, axis=-1, keepdims=True)
            hn_ref[rows, :] = (xf * lax.rsqrt(ms + NORM_EPS) * ng_ref[...]).astype(BF16)
            return carry
        lax.fori_loop(0, tm // rc, norm_rows, 0)
        gates_ref[...] = jnp.dot(hn_ref[...], wg_ref[...], preferred_element_type=F32)
        conv_silu(T_QA)

    @pl.when(j == T_KA)
    def _():
        conv_silu(T_KA)

    @pl.when(jnp.logical_or(j == T_QB, j == T_KB))
    def _():
        acc = project()
        for hh in range(d // B_HEAD_DIM):
            cols = slice(hh * B_HEAD_DIM, (hh + 1) * B_HEAD_DIM)
            blk = acc[:, cols]
            ms = jnp.mean(blk * blk, axis=-1, keepdims=True)
            proj_ref[:, cols] = (blk * lax.rsqrt(ms + NORM_EPS) * qkg_ref[0, :, cols]).astype(BF16)

    is_plain = functools.reduce(jnp.logical_or, [j == t for t in (T_VA, T_OA, T_ZA, T_VB, T_ZB, T_GA, T_GB)])

    @pl.when(is_plain)
    def _():
        proj_ref[...] = project().astype(BF16)


def _in_proj(x2d, norm_gain, w_main, w_gate, conv_w, conv_b, qk_gain, *, seq, tm):
    m, d = x2d.shape
    body = functools.partial(_in_proj_body, tm=tm, d=d, tiles_per_seq=seq // tm, rc=128,
                             k_scale=1.0 / math.sqrt(d // A_HEADS))
    return pl.pallas_call(
        body,
        grid=(m // tm, N_TILES),
        in_specs=[
            pl.BlockSpec((tm, d), lambda i, j: (i, 0)),
            pl.BlockSpec((1, d), lambda i, j: (0, 0)),
            pl.BlockSpec((d, d), lambda i, j: (0, j)),
            pl.BlockSpec((d, LANES), lambda i, j: (0, 0)),
            pl.BlockSpec((N_CONV, CONV_WIDTH, d), lambda i, j: (0, 0, 0)),
            pl.BlockSpec((N_CONV, 1, d), lambda i, j: (0, 0, 0)),
            pl.BlockSpec((1, 1, d), lambda i, j: (jnp.clip(j - T_QB, 0, 1), 0, 0)),
        ],
        out_specs=[
            pl.BlockSpec((tm, d), lambda i, j: (i, jnp.maximum(j - N_CONV, 0))),
            pl.BlockSpec((tm, d), lambda i, j: (i, jnp.minimum(j, N_CONV - 1))),
            pl.BlockSpec((tm, LANES), lambda i, j: (i, 0)),
        ],
        out_shape=[
            jax.ShapeDtypeStruct((m, (N_TILES - N_CONV) * d), BF16),
            jax.ShapeDtypeStruct((m, N_CONV * d), BF16),
            jax.ShapeDtypeStruct((m, LANES), F32),
        ],
        scratch_shapes=[
            pltpu.VMEM((tm, d), BF16),
            pltpu.VMEM((N_CONV, SUBLANES, d), F32),
        ],
        compiler_params=pltpu.CompilerParams(
            dimension_semantics=("arbitrary", "arbitrary"), vmem_limit_bytes=VMEM_LIMIT),
        name="in_proj",
    )(x2d, norm_gain, w_main, w_gate, conv_w, conv_b, qk_gain)


def _lane_scan(v, op, ident):
    lane = lax.broadcasted_iota(jnp.int32, v.shape, 1)
    shift = 1
    while shift < v.shape[1]:
        moved = pltpu.roll(v, shift, 1)
        v = op(v, jnp.where(lane >= shift, moved, ident))
        shift *= 2
    return v


def _gates_body(bias_ref, gates_ref, grow_ref, col_ref, lanes_ref, mprev_ref, *, nc):
    L = A_CHUNK
    n_gates = 2 * A_HEADS
    for c in range(nc):
        lanes_ref[c * n_gates:(c + 1) * n_gates, :] = gates_ref[c * L:(c + 1) * L, :].T[0:n_gates, :]
    for h in range(A_HEADS):
        _gates_one_head(bias_ref, lanes_ref, grow_ref, col_ref, mprev_ref, h, nc=nc)


def _gates_one_head(bias_ref, lanes_ref, grow_ref, col_ref, mprev_ref, h, *, nc):
    L = A_CHUNK
    n_gates = 2 * A_HEADS
    i_pre = lanes_ref[pl.ds(h, nc, stride=n_gates), :] + bias_ref[0, h]
    f_pre = lanes_ref[pl.ds(A_HEADS + h, nc, stride=n_gates), :] + bias_ref[1, h]
    logf = jnp.minimum(f_pre, 0.0) - jnp.log1p(jnp.exp(-jnp.abs(f_pre)))
    bcum = _lane_scan(logf, jnp.add, 0.0)
    b_last = jnp.broadcast_to(bcum[:, L - 1:L], (nc, L))
    a = b_last - bcum + i_pre
    a_max = jnp.broadcast_to(jnp.max(a, axis=-1, keepdims=True), (nc, L))
    wa = jnp.exp(a - a_max)
    g = i_pre - bcum
    g_cmax = _lane_scan(g, jnp.maximum, -jnp.inf)

    m_run = jnp.zeros((1, L), F32)
    for c in range(nc):
        mprev_ref[c:c + 1, :] = m_run
        m_run = jnp.maximum(b_last[c:c + 1, :] + m_run, a_max[c:c + 1, :])
    m_prev = mprev_ref[...]

    m_rel = jnp.maximum(m_prev, g_cmax)
    inter_w = jnp.exp(m_prev - m_rel)
    exp_neg_m = jnp.exp(-(bcum + m_rel))
    m_new = jnp.maximum(b_last + m_prev, a_max)
    decay = jnp.exp(b_last + m_prev - m_new)
    wa_scaled = wa * jnp.exp(a_max - m_new)

    grow_ref[0, h] = g
    zeros = jnp.zeros((GATE_SLOTS, L), F32)
    for blk in range(nc // GATE_SLOTS):
        rows = slice(blk * GATE_SLOTS, (blk + 1) * GATE_SLOTS)
        packed = jnp.concatenate(
            [m_rel[rows], inter_w[rows], exp_neg_m[rows], wa_scaled[rows], decay[rows], zeros, zeros, zeros],
            axis=0)
        col_ref[0, h, blk] = packed.T


def _mlstm_gates(bias, gates, *, batch, nc):
    L = A_CHUNK
    return pl.pallas_call(
        functools.partial(_gates_body, nc=nc),
        grid=(batch,),
        in_specs=[
            pl.BlockSpec(memory_space=pltpu.SMEM),
            pl.BlockSpec((nc * L, LANES), lambda bi: (bi, 0)),
        ],
        out_specs=[
            pl.BlockSpec((1, A_HEADS, nc, L), lambda bi: (bi, 0, 0, 0)),
            pl.BlockSpec((1, A_HEADS, nc // GATE_SLOTS, L, LANES), lambda bi: (bi, 0, 0, 0, 0)),
        ],
        out_shape=[
            jax.ShapeDtypeStruct((batch, A_HEADS, nc, L), F32),
            jax.ShapeDtypeStruct((batch, A_HEADS, nc // GATE_SLOTS, L, LANES), F32),
        ],
        scratch_shapes=[pltpu.VMEM((nc * 2 * A_HEADS, L), F32), pltpu.VMEM((nc, L), F32)],
        compiler_params=pltpu.CompilerParams(dimension_semantics=("parallel",)),
        name="mlstm_gates",
    )(bias, gates)


def _mlstm_body(q_ref, k_ref, v_ref, o_ref, z_ref, grow_ref, col_ref, gain_ref, out_ref, state_ref, *, hd):
    L = A_CHUNK
    cb = pl.program_id(2)

    @pl.when(cb == 0)
    def _():
        state_ref[...] = jnp.zeros_like(state_ref)

    col = col_ref[0, 0, 0]
    row_i = lax.broadcasted_iota(jnp.int32, (L, L), 0)
    col_i = lax.broadcasted_iota(jnp.int32, (L, L), 1)
    causal = col_i <= row_i
    ones_b = jnp.ones((L, LANES), BF16)
    gain = gain_ref[0]

    for c in range(GATE_SLOTS):
        rows = slice(c * L, (c + 1) * L)

        def colvec(quantity):
            lane = quantity * GATE_SLOTS + c
            return col[:, lane:lane + 1]

        m_rel, inter_w, exp_neg_m, wa_scaled = colvec(0), colvec(1), colvec(2), colvec(3)
        decay = colvec(4)[0:1, :]
        g_row = grow_ref[0, 0, c:c + 1, :]

        q = q_ref[rows, :]
        k = k_ref[rows, :]
        v_aug = jnp.concatenate([v_ref[rows, :], ones_b], axis=1)

        dmat = jnp.exp(jnp.where(causal, g_row - m_rel, -jnp.inf))
        scores = lax.dot_general(q, k, (((1,), (1,)), ((), ())), preferred_element_type=F32) * dmat
        state = state_ref[...]
        total = (jnp.dot(scores.astype(BF16), v_aug, preferred_element_type=F32)
                 + inter_w * jnp.dot(q, state.astype(BF16), preferred_element_type=F32))
        num = total[:, :hd]
        den = total[:, hd:]
        inv = 1.0 / jnp.maximum(jnp.abs(den), exp_neg_m)
        h_tilde = num * jnp.concatenate([inv] * (hd // LANES), axis=1)

        gated2 = (jnp.tanh(o_ref[rows, :].astype(F32)) + 1.0) * h_tilde
        ms2 = jnp.mean(gated2 * gated2, axis=-1, keepdims=True)
        h_a = gated2 * lax.rsqrt(ms2 + 4.0 * NORM_EPS) * gain
        out_ref[rows, :] = (h_a * _silu_of_twice(z_ref[rows, :].astype(F32))).astype(BF16)

        kw = (k.astype(F32) * wa_scaled).astype(BF16)
        update = lax.dot_general(kw, v_aug, (((0,), (0,)), ((), ())), preferred_element_type=F32)
        state_ref[...] = decay * state + update


def _mlstm_chunks(qk, proj, grow, col, head_gain, *, batch, seq, d):
    hd = d // A_HEADS
    L = A_CHUNK
    rows = GATE_SLOTS * L
    blocks_per_seq = seq // rows
    heads_per_tile = d // hd

    def col_block(tile):
        return pl.BlockSpec((rows, hd), lambda b, h, cb: (b * blocks_per_seq + cb, tile * heads_per_tile + h))

    return pl.pallas_call(
        functools.partial(_mlstm_body, hd=hd),
        grid=(batch, A_HEADS, blocks_per_seq),
        in_specs=[
            col_block(T_QA), col_block(T_KA),
            col_block(T_VA - N_CONV), col_block(T_OA - N_CONV), col_block(T_ZA - N_CONV),
            pl.BlockSpec((1, 1, GATE_SLOTS, L), lambda b, h, cb: (b, h, cb, 0)),
            pl.BlockSpec((1, 1, 1, L, LANES), lambda b, h, cb: (b, h, cb, 0, 0)),
            pl.BlockSpec((1, 1, hd), lambda b, h, cb: (h, 0, 0)),
        ],
        out_specs=pl.BlockSpec((rows, hd), lambda b, h, cb: (b * blocks_per_seq + cb, h)),
        out_shape=jax.ShapeDtypeStruct((batch * seq, d), BF16),
        scratch_shapes=[pltpu.VMEM((hd, hd + LANES), F32)],
        compiler_params=pltpu.CompilerParams(
            dimension_semantics=("parallel", "parallel", "arbitrary"), vmem_limit_bytes=VMEM_LIMIT),
        name="mlstm_chunks",
    )(qk, qk, proj, proj, proj, grow, col, head_gain)


def _stickbreak_body(q_ref, k_ref, v_ref, out_ref, *, tq, heads_per_step):
    qi = pl.program_id(2)
    hd = B_HEAD_DIM
    q0 = pl.multiple_of(qi * tq, tq)

    row_i = lax.broadcasted_iota(jnp.int32, (tq, tq), 0)
    col_i = lax.broadcasted_iota(jnp.int32, (tq, tq), 1)
    strictly_before = col_i < row_i
    suffix_op = jnp.where(row_i > col_i, 1.0, 0.0).astype(BF16)

    def scores(q, head, k0, diagonal):
        kblk = k_ref[pl.ds(k0, tq), head * hd:(head + 1) * hd]
        z2 = lax.dot_general(q, kblk, (((1,), (1,)), ((), ())), preferred_element_type=F32)
        sp2 = jnp.maximum(z2, 0.0) + jnp.log(1.0 + jnp.exp2(-jnp.abs(z2))) * LOG2_E
        if diagonal:
            sp2 = jnp.where(strictly_before, sp2, 0.0)
        return z2, sp2

    def suffix_sums(sp2_list):
        stacked = jnp.concatenate([sp2.astype(BF16) for sp2 in sp2_list], axis=0)
        local = jnp.dot(stacked, suffix_op, preferred_element_type=F32)
        return [local[n * tq:(n + 1) * tq, :] for n in range(len(sp2_list))]

    def weighted_values(z2, sp2, later, head, k0, diagonal):
        w = jnp.exp2(z2 - sp2 - later)
        if diagonal:
            w = jnp.where(strictly_before, w, 0.0)
        vblk = v_ref[pl.ds(k0, tq), head * hd:(head + 1) * hd]
        return jnp.dot(w.astype(BF16), vblk, preferred_element_type=F32)

    has_prev = jnp.where(qi > 0, 1.0, 0.0)
    prev0 = pl.multiple_of(jnp.maximum(q0 - tq, 0), tq)
    qs = [q_ref[:, head * hd:(head + 1) * hd] for head in range(heads_per_step)]
    diag = [scores(qs[head], head, q0, True) for head in range(heads_per_step)]
    prev = [scores(qs[head], head, prev0, False) for head in range(heads_per_step)]
    local = suffix_sums([sp2 for _, sp2 in diag] + [sp2 for _, sp2 in prev])
    carries = []
    for head in range(heads_per_step):
        (z2_d, sp2_d), (z2_p, sp2_p) = diag[head], prev[head]
        total_d = jnp.sum(sp2_d, axis=-1, keepdims=True)
        pv_d = weighted_values(z2_d, sp2_d, local[head], head, q0, True)
        pv_p = weighted_values(z2_p, sp2_p, local[heads_per_step + head] + total_d, head, prev0, False)
        consumed = total_d + jnp.sum(sp2_p, axis=-1, keepdims=True) * has_prev
        carries.append((consumed, pv_d + pv_p * has_prev))

    for head in range(heads_per_step):
        def more(state):
            it, consumed, _ = state
            return jnp.logical_and(it < qi - 1, jnp.min(consumed) < STICK_EXHAUSTED_LOG2)

        def step(state, head=head):
            it, consumed, acc = state
            k0 = pl.multiple_of((qi - 2 - it) * tq, tq)
            z2, sp2 = scores(qs[head], head, k0, False)
            later = suffix_sums([sp2])[0] + consumed
            acc = acc + weighted_values(z2, sp2, later, head, k0, False)
            return it + 1, consumed + jnp.sum(sp2, axis=-1, keepdims=True), acc

        _, _, acc = lax.while_loop(more, step, (jnp.int32(0),) + carries[head])
        out_ref[:, head * hd:(head + 1) * hd] = acc.astype(BF16)


def _stickbreak(proj, *, batch, seq, d, tq, heads_per_step):
    hd = B_HEAD_DIM
    groups = d // (hd * heads_per_step)
    width = hd * heads_per_step
    nq = seq // tq
    return pl.pallas_call(
        functools.partial(_stickbreak_body, tq=tq, heads_per_step=heads_per_step),
        grid=(batch, groups, nq),
        in_specs=[
            pl.BlockSpec((tq, width), lambda b, g, qi: (b * nq + qi, (T_QB - N_CONV) * groups + g)),
            pl.BlockSpec((seq, width), lambda b, g, qi: (b, (T_KB - N_CONV) * groups + g)),
            pl.BlockSpec((seq, width), lambda b, g, qi: (b, (T_VB - N_CONV) * groups + g)),
        ],
        out_specs=pl.BlockSpec((tq, width), lambda b, g, qi: (b * nq + qi, g)),
        out_shape=jax.ShapeDtypeStruct((batch * seq, d), BF16),
        compiler_params=pltpu.CompilerParams(
            dimension_semantics=("parallel", "parallel", "arbitrary"), vmem_limit_bytes=VMEM_LIMIT),
        name="stickbreak",
    )(proj, proj, proj)


def _out_proj_body(x_ref, ha_ref, ob_ref, zb_ref, ga_ref, gb_ref, wa_ref, wb_ref, wo_ref, out_ref):
    ob = (ob_ref[...].astype(F32) * _silu_of_twice(zb_ref[...].astype(F32))).astype(BF16)
    y_a = jnp.dot(ha_ref[...], wa_ref[...], preferred_element_type=F32)
    y_b = jnp.dot(ob, wb_ref[...], preferred_element_type=F32)
    merged2 = ((jnp.tanh(ga_ref[...].astype(F32)) + 1.0) * y_a
               + (jnp.tanh(gb_ref[...].astype(F32)) + 1.0) * y_b)
    out_ref[...] = x_ref[...] + jnp.dot(merged2.astype(BF16), wo_ref[...], preferred_element_type=F32)


def _out_proj(x2d, ha, ob, proj, w_a, w_b, w_o_half, *, tm):
    m, d = x2d.shape
    row_tile = pl.BlockSpec((tm, d), lambda i: (i, 0))
    weight = pl.BlockSpec((d, d), lambda i: (0, 0))

    def proj_tile(tile):
        return pl.BlockSpec((tm, d), lambda i: (i, tile - N_CONV))

    return pl.pallas_call(
        _out_proj_body,
        grid=(m // tm,),
        in_specs=[row_tile, row_tile, row_tile, proj_tile(T_ZB), proj_tile(T_GA), proj_tile(T_GB),
                  weight, weight, weight],
        out_specs=row_tile,
        out_shape=jax.ShapeDtypeStruct((m, d), F32),
        compiler_params=pltpu.CompilerParams(
            dimension_semantics=("parallel",), vmem_limit_bytes=VMEM_LIMIT),
        name="out_proj",
    )(x2d, ha, ob, proj, proj, proj, w_a, w_b, w_o_half)


def _layer(x, norm_gain, w_in, conv_w, conv_b, i_bias, f_bias, a_head_gain, qk_gain_q, qk_gain_k,
           w_branch_a, w_branch_b, w_out):
    batch, seq, d = x.shape
    m = batch * seq
    nc = seq // A_CHUNK
    hd_a = d // A_HEADS
    b_heads = d // B_HEAD_DIM
    assert seq % (GATE_SLOTS * A_CHUNK) == 0 and d % LANES == 0

    gate_lo = 5 * d
    gate_hi = gate_lo + 2 * A_HEADS
    tile_scale = jnp.ones((N_TILES,), F32).at[jnp.array(HALVED_TILES)].set(0.5)
    w_main = jnp.concatenate([w_in[:, :gate_lo], w_in[:, gate_hi:]], axis=1)
    w_main = (w_main * jnp.repeat(tile_scale, d)[None, :]).astype(BF16)
    w_gate = jnp.pad(w_in[:, gate_lo:gate_hi], ((0, 0), (0, LANES - 2 * A_HEADS))).astype(BF16)
    qk_gain = jnp.stack([jnp.tile(qk_gain_q, b_heads) * (LOG2_E / math.sqrt(B_HEAD_DIM)),
                         jnp.tile(qk_gain_k, b_heads)])[:, None, :]
    conv_w_half = 0.5 * conv_w.reshape(CONV_WIDTH, N_CONV, d).transpose(1, 0, 2)
    conv_b_half = 0.5 * conv_b.reshape(N_CONV, 1, d)

    x2d = x.reshape(m, d)
    proj, qk, gates = _in_proj(x2d, norm_gain[None, :], w_main, w_gate, conv_w_half, conv_b_half, qk_gain,
                               seq=seq, tm=min(1024, seq))

    grow, col = _mlstm_gates(jnp.stack([i_bias, f_bias]), gates, batch=batch, nc=nc)
    ha = _mlstm_chunks(qk, proj, grow, col, a_head_gain.reshape(A_HEADS, 1, hd_a), batch=batch, seq=seq, d=d)

    ob = _stickbreak(proj, batch=batch, seq=seq, d=d, tq=256, heads_per_step=4)

    out = _out_proj(x2d, ha, ob, proj, w_branch_a.astype(BF16), w_branch_b.astype(BF16),
                    (0.5 * w_out).astype(BF16), tm=512)
    return out.reshape(batch, seq, d)


def kernel(x, norm_gain, w_in, conv_w, conv_b, i_bias, f_bias, a_head_gain, qk_gain_q, qk_gain_k,
           w_branch_a, w_branch_b, w_out):
    for layer in range(norm_gain.shape[0]):
        x = _layer(x, norm_gain[layer], w_in[layer], conv_w[layer], conv_b[layer], i_bias[layer],
                   f_bias[layer], a_head_gain[layer], qk_gain_q[layer], qk_gain_k[layer],
                   w_branch_a[layer], w_branch_b[layer], w_out[layer])
    return x
```

```python
import functools
import math

import jax
import jax.numpy as jnp
from jax import lax
from jax.experimental import pallas as pl
from jax.experimental.pallas import tpu as pltpu

F32 = jnp.float32
BF16 = jnp.bfloat16

NORM_EPS = 1e-6
A_HEADS = 4
A_CHUNK = 128
CONV_WIDTH = 4
B_HEAD_DIM = 128
LANES = 128
SUBLANES = 8
VMEM_LIMIT = 56 * 1024 * 1024

T_QA, T_KA, T_VA, T_OA, T_ZA, T_QB, T_KB, T_VB, T_ZB, T_GA, T_GB = range(11)
N_TILES = 11
N_CONV = 2
HALVED_TILES = (T_OA, T_ZA, T_ZB, T_GA, T_GB)
GATE_SLOTS = 16
LOG2_E = 1.0 / math.log(2.0)
STICK_EXHAUSTED_LOG2 = 160.0


def _silu_of_twice(u):
    return u * (jnp.tanh(u) + 1.0)


def _in_proj_body(x_ref, ng_ref, w_ref, wg_ref, cw_ref, cb_ref, qkg_ref, proj_ref, qk_ref, gates_ref,
                  hn_ref, halo_ref, *, tm, d, tiles_per_seq, rc, k_scale):
    i = pl.program_id(0)
    j = pl.program_id(1)
    first = (i % tiles_per_seq) == 0

    def project():
        return jnp.dot(hn_ref[...], w_ref[...], preferred_element_type=F32)

    def conv_silu(slot):
        acc = project()
        window = jnp.concatenate([jnp.where(first, 0.0, halo_ref[slot]), acc], axis=0)
        halo_ref[slot] = acc[tm - SUBLANES:, :]
        u = cw_ref[slot, 0:1, :] * window
        for tap in range(1, CONV_WIDTH):
            u = cw_ref[slot, tap:tap + 1, :] * window + pltpu.roll(u, 1, 0)
        y = _silu_of_twice(u[SUBLANES:, :] + cb_ref[slot])
        qk_ref[...] = (y * k_scale if slot == T_KA else y).astype(BF16)

    @pl.when(j == T_QA)
    def _():
        def norm_rows(r, carry):
            rows = pl.ds(pl.multiple_of(r * rc, rc), rc)
            xf = x_ref[rows, :]
            ms = jnp.mean(xf * xf, axis=-1, keepdims=True)
            hn_ref[rows, :] = (xf * lax.rsqrt(ms + NORM_EPS) * ng_ref[...]).astype(BF16)
            return carry
        lax.fori_loop(0, tm // rc, norm_rows, 0)
        gates_ref[...] = jnp.dot(hn_ref[...], wg_ref[...], preferred_element_type=F32)
        conv_silu(T_QA)

    @pl.when(j == T_KA)
    def _():
        conv_silu(T_KA)

    @pl.when(jnp.logical_or(j == T_QB, j == T_KB))
    def _():
        acc = project()
        for hh in range(d // B_HEAD_DIM):
            cols = slice(hh * B_HEAD_DIM, (hh + 1) * B_HEAD_DIM)
            blk = acc[:, cols]
            ms = jnp.mean(blk * blk, axis=-1, keepdims=True)
            proj_ref[:, cols] = (blk * lax.rsqrt(ms + NORM_EPS) * qkg_ref[0, :, cols]).astype(BF16)

    is_plain = functools.reduce(jnp.logical_or, [j == t for t in (T_VA, T_OA, T_ZA, T_VB, T_ZB, T_GA, T_GB)])

    @pl.when(is_plain)
    def _():
        proj_ref[...] = project().astype(BF16)


def _in_proj(x2d, norm_gain, w_main, w_gate, conv_w, conv_b, qk_gain, *, seq, tm):
    m, d = x2d.shape
    body = functools.partial(_in_proj_body, tm=tm, d=d, tiles_per_seq=seq // tm, rc=256,
                             k_scale=1.0 / math.sqrt(d // A_HEADS))
    return pl.pallas_call(
        body,
        grid=(m // tm, N_TILES),
        in_specs=[
            pl.BlockSpec((tm, d), lambda i, j: (i, 0)),
            pl.BlockSpec((1, d), lambda i, j: (0, 0)),
            pl.BlockSpec((d, d), lambda i, j: (0, j)),
            pl.BlockSpec((d, LANES), lambda i, j: (0, 0)),
            pl.BlockSpec((N_CONV, CONV_WIDTH, d), lambda i, j: (0, 0, 0)),
            pl.BlockSpec((N_CONV, 1, d), lambda i, j: (0, 0, 0)),
            pl.BlockSpec((1, 1, d), lambda i, j: (jnp.clip(j - T_QB, 0, 1), 0, 0)),
        ],
        out_specs=[
            pl.BlockSpec((tm, d), lambda i, j: (i, jnp.maximum(j - N_CONV, 0))),
            pl.BlockSpec((tm, d), lambda i, j: (i, jnp.minimum(j, N_CONV - 1))),
            pl.BlockSpec((tm, LANES), lambda i, j: (i, 0)),
        ],
        out_shape=[
            jax.ShapeDtypeStruct((m, (N_TILES - N_CONV) * d), BF16),
            jax.ShapeDtypeStruct((m, N_CONV * d), BF16),
            jax.ShapeDtypeStruct((m, LANES), F32),
        ],
        scratch_shapes=[
            pltpu.VMEM((tm, d), BF16),
            pltpu.VMEM((N_CONV, SUBLANES, d), F32),
        ],
        compiler_params=pltpu.CompilerParams(
            dimension_semantics=("arbitrary", "arbitrary"), vmem_limit_bytes=VMEM_LIMIT),
        name="in_proj",
    )(x2d, norm_gain, w_main, w_gate, conv_w, conv_b, qk_gain)


def _lane_scan(v, op, ident):
    lane = lax.broadcasted_iota(jnp.int32, v.shape, 1)
    shift = 1
    while shift < v.shape[1]:
        moved = pltpu.roll(v, shift, 1)
        v = op(v, jnp.where(lane >= shift, moved, ident))
        shift *= 2
    return v


def _gates_body(bias_ref, gates_ref, grow_ref, col_ref, lanes_ref, mprev_ref, *, nc):
    L = A_CHUNK
    n_gates = 2 * A_HEADS
    for c in range(nc):
        lanes_ref[c * n_gates:(c + 1) * n_gates, :] = gates_ref[c * L:(c + 1) * L, :].T[0:n_gates, :]
    for h in range(A_HEADS):
        _gates_one_head(bias_ref, lanes_ref, grow_ref, col_ref, mprev_ref, h, nc=nc)


def _gates_one_head(bias_ref, lanes_ref, grow_ref, col_ref, mprev_ref, h, *, nc):
    L = A_CHUNK
    n_gates = 2 * A_HEADS
    i_pre = lanes_ref[pl.ds(h, nc, stride=n_gates), :] + bias_ref[0, h]
    f_pre = lanes_ref[pl.ds(A_HEADS + h, nc, stride=n_gates), :] + bias_ref[1, h]
    logf = jnp.minimum(f_pre, 0.0) - jnp.log1p(jnp.exp(-jnp.abs(f_pre)))
    bcum = _lane_scan(logf, jnp.add, 0.0)
    b_last = jnp.broadcast_to(bcum[:, L - 1:L], (nc, L))
    a = b_last - bcum + i_pre
    a_max = jnp.broadcast_to(jnp.max(a, axis=-1, keepdims=True), (nc, L))
    wa = jnp.exp(a - a_max)
    g = i_pre - bcum
    g_cmax = _lane_scan(g, jnp.maximum, -jnp.inf)

    m_run = jnp.zeros((1, L), F32)
    for c in range(nc):
        mprev_ref[c:c + 1, :] = m_run
        m_run = jnp.maximum(b_last[c:c + 1, :] + m_run, a_max[c:c + 1, :])
    m_prev = mprev_ref[...]

    m_rel = jnp.maximum(m_prev, g_cmax)
    inter_w = jnp.exp(m_prev - m_rel)
    exp_neg_m = jnp.exp(-(bcum + m_rel))
    m_new = jnp.maximum(b_last + m_prev, a_max)
    decay = jnp.exp(b_last + m_prev - m_new)
    wa_scaled = wa * jnp.exp(a_max - m_new)

    grow_ref[0, h] = g
    zeros = jnp.zeros((GATE_SLOTS, L), F32)
    for blk in range(nc // GATE_SLOTS):
        rows = slice(blk * GATE_SLOTS, (blk + 1) * GATE_SLOTS)
        packed = jnp.concatenate(
            [m_rel[rows], inter_w[rows], exp_neg_m[rows], wa_scaled[rows], decay[rows], zeros, zeros, zeros],
            axis=0)
        col_ref[0, h, blk] = packed.T


def _mlstm_gates(bias, gates, *, batch, nc):
    L = A_CHUNK
    return pl.pallas_call(
        functools.partial(_gates_body, nc=nc),
        grid=(batch,),
        in_specs=[
            pl.BlockSpec(memory_space=pltpu.SMEM),
            pl.BlockSpec((nc * L, LANES), lambda bi: (bi, 0)),
        ],
        out_specs=[
            pl.BlockSpec((1, A_HEADS, nc, L), lambda bi: (bi, 0, 0, 0)),
            pl.BlockSpec((1, A_HEADS, nc // GATE_SLOTS, L, LANES), lambda bi: (bi, 0, 0, 0, 0)),
        ],
        out_shape=[
            jax.ShapeDtypeStruct((batch, A_HEADS, nc, L), F32),
            jax.ShapeDtypeStruct((batch, A_HEADS, nc // GATE_SLOTS, L, LANES), F32),
        ],
        scratch_shapes=[pltpu.VMEM((nc * 2 * A_HEADS, L), F32), pltpu.VMEM((nc, L), F32)],
        compiler_params=pltpu.CompilerParams(dimension_semantics=("parallel",)),
        name="mlstm_gates",
    )(bias, gates)


def _mlstm_body(q_ref, k_ref, v_ref, o_ref, z_ref, grow_ref, col_ref, gain_ref, out_ref, state_ref, *, hd):
    L = A_CHUNK
    cb = pl.program_id(2)

    @pl.when(cb == 0)
    def _():
        state_ref[...] = jnp.zeros_like(state_ref)

    col = col_ref[0, 0, 0]
    row_i = lax.broadcasted_iota(jnp.int32, (L, L), 0)
    col_i = lax.broadcasted_iota(jnp.int32, (L, L), 1)
    causal = col_i <= row_i
    ones_b = jnp.ones((L, LANES), BF16)
    gain = gain_ref[0]

    for c in range(GATE_SLOTS):
        rows = slice(c * L, (c + 1) * L)

        def colvec(quantity):
            lane = quantity * GATE_SLOTS + c
            return col[:, lane:lane + 1]

        m_rel, inter_w, exp_neg_m, wa_scaled = colvec(0), colvec(1), colvec(2), colvec(3)
        decay = colvec(4)[0:1, :]
        g_row = grow_ref[0, 0, c:c + 1, :]

        q = q_ref[rows, :]
        k = k_ref[rows, :]
        v_aug = jnp.concatenate([v_ref[rows, :], ones_b], axis=1)

        dmat = jnp.exp(jnp.where(causal, g_row - m_rel, -jnp.inf))
        scores = lax.dot_general(q, k, (((1,), (1,)), ((), ())), preferred_element_type=F32) * dmat
        state = state_ref[...]
        total = (jnp.dot(scores.astype(BF16), v_aug, preferred_element_type=F32)
                 + inter_w * jnp.dot(q, state.astype(BF16), preferred_element_type=F32))
        num = total[:, :hd]
        den = total[:, hd:]
        inv = 1.0 / jnp.maximum(jnp.abs(den), exp_neg_m)
        h_tilde = num * jnp.concatenate([inv] * (hd // LANES), axis=1)

        gated2 = (jnp.tanh(o_ref[rows, :].astype(F32)) + 1.0) * h_tilde
        ms2 = jnp.mean(gated2 * gated2, axis=-1, keepdims=True)
        h_a = gated2 * lax.rsqrt(ms2 + 4.0 * NORM_EPS) * gain
        out_ref[rows, :] = (h_a * _silu_of_twice(z_ref[rows, :].astype(F32))).astype(BF16)

        kw = (k.astype(F32) * wa_scaled).astype(BF16)
        update = lax.dot_general(kw, v_aug, (((0,), (0,)), ((), ())), preferred_element_type=F32)
        state_ref[...] = decay * state + update


def _mlstm_chunks(qk, proj, grow, col, head_gain, *, batch, seq, d):
    hd = d // A_HEADS
    L = A_CHUNK
    rows = GATE_SLOTS * L
    blocks_per_seq = seq // rows
    heads_per_tile = d // hd

    def col_block(tile):
        return pl.BlockSpec((rows, hd), lambda b, h, cb: (b * blocks_per_seq + cb, tile * heads_per_tile + h))

    return pl.pallas_call(
        functools.partial(_mlstm_body, hd=hd),
        grid=(batch, A_HEADS, blocks_per_seq),
        in_specs=[
            col_block(T_QA), col_block(T_KA),
            col_block(T_VA - N_CONV), col_block(T_OA - N_CONV), col_block(T_ZA - N_CONV),
            pl.BlockSpec((1, 1, GATE_SLOTS, L), lambda b, h, cb: (b, h, cb, 0)),
            pl.BlockSpec((1, 1, 1, L, LANES), lambda b, h, cb: (b, h, cb, 0, 0)),
            pl.BlockSpec((1, 1, hd), lambda b, h, cb: (h, 0, 0)),
        ],
        out_specs=pl.BlockSpec((rows, hd), lambda b, h, cb: (b * blocks_per_seq + cb, h)),
        out_shape=jax.ShapeDtypeStruct((batch * seq, d), BF16),
        scratch_shapes=[pltpu.VMEM((hd, hd + LANES), F32)],
        compiler_params=pltpu.CompilerParams(
            dimension_semantics=("parallel", "parallel", "arbitrary"), vmem_limit_bytes=VMEM_LIMIT),
        name="mlstm_chunks",
    )(qk, qk, proj, proj, proj, grow, col, head_gain)


def _stickbreak_body(q_ref, k_ref, v_ref, out_ref, *, tq, heads_per_step):
    qi = pl.program_id(2)
    hd = B_HEAD_DIM
    q0 = pl.multiple_of(qi * tq, tq)

    row_i = lax.broadcasted_iota(jnp.int32, (tq, tq), 0)
    col_i = lax.broadcasted_iota(jnp.int32, (tq, tq), 1)
    strictly_before = col_i < row_i
    suffix_op = jnp.where(row_i > col_i, 1.0, 0.0).astype(BF16)

    def scores(q, head, k0, diagonal):
        kblk = k_ref[pl.ds(k0, tq), head * hd:(head + 1) * hd]
        z2 = lax.dot_general(q, kblk, (((1,), (1,)), ((), ())), preferred_element_type=F32)
        sp2 = jnp.maximum(z2, 0.0) + jnp.log(1.0 + jnp.exp2(-jnp.abs(z2))) * LOG2_E
        if diagonal:
            sp2 = jnp.where(strictly_before, sp2, 0.0)
        return z2, sp2

    def suffix_sums(sp2_list):
        stacked = jnp.concatenate([sp2.astype(BF16) for sp2 in sp2_list], axis=0)
        local = jnp.dot(stacked, suffix_op, preferred_element_type=F32)
        return [local[n * tq:(n + 1) * tq, :] for n in range(len(sp2_list))]

    def weights(z2, sp2, later, diagonal):
        w = jnp.exp2(z2 - sp2 - later)
        if diagonal:
            w = jnp.where(strictly_before, w, 0.0)
        return w.astype(BF16)

    def values(head, k0):
        return v_ref[pl.ds(k0, tq), head * hd:(head + 1) * hd]

    has_prev = jnp.where(qi > 0, 1.0, 0.0)
    has_prev_b = has_prev.astype(BF16)
    prev0 = pl.multiple_of(jnp.maximum(q0 - tq, 0), tq)
    qs = [q_ref[:, head * hd:(head + 1) * hd] for head in range(heads_per_step)]
    diag = [scores(qs[head], head, q0, True) for head in range(heads_per_step)]
    prev = [scores(qs[head], head, prev0, False) for head in range(heads_per_step)]
    local = suffix_sums([sp2 for _, sp2 in diag] + [sp2 for _, sp2 in prev])
    carries = []
    for head in range(heads_per_step):
        (z2_d, sp2_d), (z2_p, sp2_p) = diag[head], prev[head]
        total_d = jnp.sum(sp2_d, axis=-1, keepdims=True)
        w_both = jnp.concatenate([weights(z2_d, sp2_d, local[head], True),
                                  weights(z2_p, sp2_p, local[heads_per_step + head] + total_d, False)], axis=1)
        v_both = jnp.concatenate([values(head, q0), values(head, prev0) * has_prev_b], axis=0)
        consumed = total_d + jnp.sum(sp2_p, axis=-1, keepdims=True) * has_prev
        carries.append((consumed, jnp.dot(w_both, v_both, preferred_element_type=F32)))

    for head in range(heads_per_step):
        def more(state):
            it, consumed, _ = state
            return jnp.logical_and(it < qi - 1, jnp.min(consumed) < STICK_EXHAUSTED_LOG2)

        def step(state, head=head):
            it, consumed, acc = state
            k0 = pl.multiple_of((qi - 2 - it) * tq, tq)
            z2, sp2 = scores(qs[head], head, k0, False)
            later = suffix_sums([sp2])[0] + consumed
            acc = acc + jnp.dot(weights(z2, sp2, later, False), values(head, k0), preferred_element_type=F32)
            return it + 1, consumed + jnp.sum(sp2, axis=-1, keepdims=True), acc

        _, _, acc = lax.while_loop(more, step, (jnp.int32(0),) + carries[head])
        out_ref[:, head * hd:(head + 1) * hd] = acc.astype(BF16)


def _stickbreak(proj, *, batch, seq, d, tq, heads_per_step):
    hd = B_HEAD_DIM
    groups = d // (hd * heads_per_step)
    width = hd * heads_per_step
    nq = seq // tq
    return pl.pallas_call(
        functools.partial(_stickbreak_body, tq=tq, heads_per_step=heads_per_step),
        grid=(batch, groups, nq),
        in_specs=[
            pl.BlockSpec((tq, width), lambda b, g, qi: (b * nq + qi, (T_QB - N_CONV) * groups + g)),
            pl.BlockSpec((seq, width), lambda b, g, qi: (b, (T_KB - N_CONV) * groups + g)),
            pl.BlockSpec((seq, width), lambda b, g, qi: (b, (T_VB - N_CONV) * groups + g)),
        ],
        out_specs=pl.BlockSpec((tq, width), lambda b, g, qi: (b * nq + qi, g)),
        out_shape=jax.ShapeDtypeStruct((batch * seq, d), BF16),
        compiler_params=pltpu.CompilerParams(
            dimension_semantics=("parallel", "parallel", "arbitrary"), vmem_limit_bytes=VMEM_LIMIT),
        name="stickbreak",
    )(proj, proj, proj)


def _out_proj_body(x_ref, ha_ref, ob_ref, zb_ref, ga_ref, gb_ref, wa_ref, wb_ref, wo_ref, out_ref):
    ob = (ob_ref[...].astype(F32) * _silu_of_twice(zb_ref[...].astype(F32))).astype(BF16)
    y_a = jnp.dot(ha_ref[...], wa_ref[...], preferred_element_type=F32)
    y_b = jnp.dot(ob, wb_ref[...], preferred_element_type=F32)
    merged2 = ((jnp.tanh(ga_ref[...].astype(F32)) + 1.0) * y_a
               + (jnp.tanh(gb_ref[...].astype(F32)) + 1.0) * y_b)
    out_ref[...] = x_ref[...] + jnp.dot(merged2.astype(BF16), wo_ref[...], preferred_element_type=F32)


def _out_proj(x2d, ha, ob, proj, w_a, w_b, w_o_half, *, tm):
    m, d = x2d.shape
    row_tile = pl.BlockSpec((tm, d), lambda i: (i, 0))
    weight = pl.BlockSpec((d, d), lambda i: (0, 0))

    def proj_tile(tile):
        return pl.BlockSpec((tm, d), lambda i: (i, tile - N_CONV))

    return pl.pallas_call(
        _out_proj_body,
        grid=(m // tm,),
        in_specs=[row_tile, row_tile, row_tile, proj_tile(T_ZB), proj_tile(T_GA), proj_tile(T_GB),
                  weight, weight, weight],
        out_specs=row_tile,
        out_shape=jax.ShapeDtypeStruct((m, d), F32),
        compiler_params=pltpu.CompilerParams(
            dimension_semantics=("parallel",), vmem_limit_bytes=VMEM_LIMIT),
        name="out_proj",
    )(x2d, ha, ob, proj, proj, proj, w_a, w_b, w_o_half)


def _layer(x, norm_gain, w_in, conv_w, conv_b, i_bias, f_bias, a_head_gain, qk_gain_q, qk_gain_k,
           w_branch_a, w_branch_b, w_out):
    batch, seq, d = x.shape
    m = batch * seq
    nc = seq // A_CHUNK
    hd_a = d // A_HEADS
    b_heads = d // B_HEAD_DIM
    assert seq % (GATE_SLOTS * A_CHUNK) == 0 and d % LANES == 0

    gate_lo = 5 * d
    gate_hi = gate_lo + 2 * A_HEADS
    tile_scale = jnp.ones((N_TILES,), F32).at[jnp.array(HALVED_TILES)].set(0.5)
    w_bf = w_in.astype(BF16)
    w_main = jnp.concatenate([w_bf[:, :gate_lo], w_bf[:, gate_hi:]], axis=1)
    w_main = w_main * jnp.repeat(tile_scale, d).astype(BF16)[None, :]
    w_gate = jnp.pad(w_bf[:, gate_lo:gate_hi], ((0, 0), (0, LANES - 2 * A_HEADS)))
    qk_gain = jnp.stack([jnp.tile(qk_gain_q, b_heads) * (LOG2_E / math.sqrt(B_HEAD_DIM)),
                         jnp.tile(qk_gain_k, b_heads)])[:, None, :]
    conv_w_half = 0.5 * conv_w.reshape(CONV_WIDTH, N_CONV, d).transpose(1, 0, 2)
    conv_b_half = 0.5 * conv_b.reshape(N_CONV, 1, d)

    x2d = x.reshape(m, d)
    proj, qk, gates = _in_proj(x2d, norm_gain[None, :], w_main, w_gate, conv_w_half, conv_b_half, qk_gain,
                               seq=seq, tm=min(1024, seq))

    grow, col = _mlstm_gates(jnp.stack([i_bias, f_bias]), gates, batch=batch, nc=nc)
    ha = _mlstm_chunks(qk, proj, grow, col, a_head_gain.reshape(A_HEADS, 1, hd_a), batch=batch, seq=seq, d=d)

    ob = _stickbreak(proj, batch=batch, seq=seq, d=d, tq=256, heads_per_step=4)

    out = _out_proj(x2d, ha, ob, proj, w_branch_a.astype(BF16), w_branch_b.astype(BF16),
                    (0.5 * w_out).astype(BF16), tm=512)
    return out.reshape(batch, seq, d)


def kernel(x, norm_gain, w_in, conv_w, conv_b, i_bias, f_bias, a_head_gain, qk_gain_q, qk_gain_k,
           w_branch_a, w_branch_b, w_out):
    for layer in range(norm_gain.shape[0]):
        x = _layer(x, norm_gain[layer], w_in[layer], conv_w[layer], conv_b[layer], i_bias[layer],
                   f_bias[layer], a_head_gain[layer], qk_gain_q[layer], qk_gain_k[layer],
                   w_branch_a[layer], w_branch_b[layer], w_out[layer])
    return x
```

```python
import functools
import math

import jax
import jax.numpy as jnp
from jax import lax
from jax.experimental import pallas as pl
from jax.experimental.pallas import tpu as pltpu

F32 = jnp.float32
BF16 = jnp.bfloat16

NORM_EPS = 1e-6
A_HEADS = 4
A_CHUNK = 128
CONV_WIDTH = 4
B_HEAD_DIM = 128
LANES = 128
SUBLANES = 8
VMEM_LIMIT = 56 * 1024 * 1024

T_QA, T_KA, T_VA, T_OA, T_ZA, T_QB, T_KB, T_VB, T_ZB, T_GA, T_GB = range(11)
N_TILES = 11
N_CONV = 2
HALVED_TILES = (T_OA, T_ZA, T_ZB, T_GA, T_GB)
GATE_SLOTS = 16
LOG2_E = 1.0 / math.log(2.0)
STICK_EXHAUSTED_LOG2 = 160.0


def _silu_of_twice(u):
    return u * (jnp.tanh(u) + 1.0)


def _in_proj_body(x_ref, ng_ref, w_ref, wg_ref, cw_ref, cb_ref, qkg_ref, proj_ref, qk_ref, gates_ref,
                  hn_ref, halo_ref, *, tm, d, tiles_per_seq, rc, k_scale):
    i = pl.program_id(0)
    j = pl.program_id(1)
    first = (i % tiles_per_seq) == 0

    def project():
        return jnp.dot(hn_ref[...], w_ref[...], preferred_element_type=F32)

    def conv_silu(slot):
        acc = project()
        window = jnp.concatenate([jnp.where(first, 0.0, halo_ref[slot]), acc], axis=0)
        halo_ref[slot] = acc[tm - SUBLANES:, :]
        u = cw_ref[slot, 0:1, :] * window
        for tap in range(1, CONV_WIDTH):
            u = cw_ref[slot, tap:tap + 1, :] * window + pltpu.roll(u, 1, 0)
        y = _silu_of_twice(u[SUBLANES:, :] + cb_ref[slot])
        qk_ref[...] = (y * k_scale if slot == T_KA else y).astype(BF16)

    @pl.when(j == T_QA)
    def _():
        def norm_rows(r, carry):
            rows = pl.ds(pl.multiple_of(r * rc, rc), rc)
            xf = x_ref[rows, :]
            ms = jnp.mean(xf * xf, axis=-1, keepdims=True)
            hn_ref[rows, :] = (xf * lax.rsqrt(ms + NORM_EPS) * ng_ref[...]).astype(BF16)
            return carry
        lax.fori_loop(0, tm // rc, norm_rows, 0)
        gates_ref[...] = jnp.dot(hn_ref[...], wg_ref[...], preferred_element_type=F32)
        conv_silu(T_QA)

    @pl.when(j == T_KA)
    def _():
        conv_silu(T_KA)

    @pl.when(jnp.logical_or(j == T_QB, j == T_KB))
    def _():
        acc = project()
        for hh in range(d // B_HEAD_DIM):
            cols = slice(hh * B_HEAD_DIM, (hh + 1) * B_HEAD_DIM)
            blk = acc[:, cols]
            ms = jnp.mean(blk * blk, axis=-1, keepdims=True)
            proj_ref[:, cols] = (blk * lax.rsqrt(ms + NORM_EPS) * qkg_ref[0, :, cols]).astype(BF16)

    is_plain = functools.reduce(jnp.logical_or, [j == t for t in (T_VA, T_OA, T_ZA, T_VB, T_ZB, T_GA, T_GB)])

    @pl.when(is_plain)
    def _():
        proj_ref[...] = project().astype(BF16)


def _in_proj(x2d, norm_gain, w_main, w_gate, conv_w, conv_b, qk_gain, *, seq, tm):
    m, d = x2d.shape
    body = functools.partial(_in_proj_body, tm=tm, d=d, tiles_per_seq=seq // tm, rc=256,
                             k_scale=1.0 / math.sqrt(d // A_HEADS))
    return pl.pallas_call(
        body,
        grid=(m // tm, N_TILES),
        in_specs=[
            pl.BlockSpec((tm, d), lambda i, j: (i, 0)),
            pl.BlockSpec((1, d), lambda i, j: (0, 0)),
            pl.BlockSpec((d, d), lambda i, j: (0, j)),
            pl.BlockSpec((d, LANES), lambda i, j: (0, 0)),
            pl.BlockSpec((N_CONV, CONV_WIDTH, d), lambda i, j: (0, 0, 0)),
            pl.BlockSpec((N_CONV, 1, d), lambda i, j: (0, 0, 0)),
            pl.BlockSpec((1, 1, d), lambda i, j: (jnp.clip(j - T_QB, 0, 1), 0, 0)),
        ],
        out_specs=[
            pl.BlockSpec((tm, d), lambda i, j: (i, jnp.maximum(j - N_CONV, 0))),
            pl.BlockSpec((tm, d), lambda i, j: (i, jnp.minimum(j, N_CONV - 1))),
            pl.BlockSpec((tm, LANES), lambda i, j: (i, 0)),
        ],
        out_shape=[
            jax.ShapeDtypeStruct((m, (N_TILES - N_CONV) * d), BF16),
            jax.ShapeDtypeStruct((m, N_CONV * d), BF16),
            jax.ShapeDtypeStruct((m, LANES), F32),
        ],
        scratch_shapes=[
            pltpu.VMEM((tm, d), BF16),
            pltpu.VMEM((N_CONV, SUBLANES, d), F32),
        ],
        compiler_params=pltpu.CompilerParams(
            dimension_semantics=("arbitrary", "arbitrary"), vmem_limit_bytes=VMEM_LIMIT),
        name="in_proj",
    )(x2d, norm_gain, w_main, w_gate, conv_w, conv_b, qk_gain)


def _lane_scan(v, op, ident):
    lane = lax.broadcasted_iota(jnp.int32, v.shape, 1)
    shift = 1
    while shift < v.shape[1]:
        moved = pltpu.roll(v, shift, 1)
        v = op(v, jnp.where(lane >= shift, moved, ident))
        shift *= 2
    return v


def _gates_body(bias_ref, gates_ref, grow_ref, col_ref, lanes_ref, mprev_ref, *, nc):
    L = A_CHUNK
    n_gates = 2 * A_HEADS
    for c in range(nc):
        lanes_ref[c * n_gates:(c + 1) * n_gates, :] = gates_ref[c * L:(c + 1) * L, :].T[0:n_gates, :]
    for h in range(A_HEADS):
        _gates_one_head(bias_ref, lanes_ref, grow_ref, col_ref, mprev_ref, h, nc=nc)


def _gates_one_head(bias_ref, lanes_ref, grow_ref, col_ref, mprev_ref, h, *, nc):
    L = A_CHUNK
    n_gates = 2 * A_HEADS
    i_pre = lanes_ref[pl.ds(h, nc, stride=n_gates), :] + bias_ref[0, h]
    f_pre = lanes_ref[pl.ds(A_HEADS + h, nc, stride=n_gates), :] + bias_ref[1, h]
    logf = jnp.minimum(f_pre, 0.0) - jnp.log1p(jnp.exp(-jnp.abs(f_pre)))
    bcum = _lane_scan(logf, jnp.add, 0.0)
    b_last = jnp.broadcast_to(bcum[:, L - 1:L], (nc, L))
    a = b_last - bcum + i_pre
    a_max = jnp.broadcast_to(jnp.max(a, axis=-1, keepdims=True), (nc, L))
    wa = jnp.exp(a - a_max)
    g = i_pre - bcum
    g_cmax = _lane_scan(g, jnp.maximum, -jnp.inf)

    m_run = jnp.zeros((1, L), F32)
    for c in range(nc):
        mprev_ref[c:c + 1, :] = m_run
        m_run = jnp.maximum(b_last[c:c + 1, :] + m_run, a_max[c:c + 1, :])
    m_prev = mprev_ref[...]

    m_rel = jnp.maximum(m_prev, g_cmax)
    inter_w = jnp.exp(m_prev - m_rel)
    exp_neg_m = jnp.exp(-(bcum + m_rel))
    m_new = jnp.maximum(b_last + m_prev, a_max)
    decay = jnp.exp(b_last + m_prev - m_new)
    wa_scaled = wa * jnp.exp(a_max - m_new)

    grow_ref[0, h] = g
    zeros = jnp.zeros((GATE_SLOTS, L), F32)
    for blk in range(nc // GATE_SLOTS):
        rows = slice(blk * GATE_SLOTS, (blk + 1) * GATE_SLOTS)
        packed = jnp.concatenate(
            [m_rel[rows], inter_w[rows], exp_neg_m[rows], wa_scaled[rows], decay[rows], zeros, zeros, zeros],
            axis=0)
        col_ref[0, h, blk] = packed.T


def _mlstm_gates(bias, gates, *, batch, nc):
    L = A_CHUNK
    return pl.pallas_call(
        functools.partial(_gates_body, nc=nc),
        grid=(batch,),
        in_specs=[
            pl.BlockSpec(memory_space=pltpu.SMEM),
            pl.BlockSpec((nc * L, LANES), lambda bi: (bi, 0)),
        ],
        out_specs=[
            pl.BlockSpec((1, A_HEADS, nc, L), lambda bi: (bi, 0, 0, 0)),
            pl.BlockSpec((1, A_HEADS, nc // GATE_SLOTS, L, LANES), lambda bi: (bi, 0, 0, 0, 0)),
        ],
        out_shape=[
            jax.ShapeDtypeStruct((batch, A_HEADS, nc, L), F32),
            jax.ShapeDtypeStruct((batch, A_HEADS, nc // GATE_SLOTS, L, LANES), F32),
        ],
        scratch_shapes=[pltpu.VMEM((nc * 2 * A_HEADS, L), F32), pltpu.VMEM((nc, L), F32)],
        compiler_params=pltpu.CompilerParams(dimension_semantics=("parallel",)),
        name="mlstm_gates",
    )(bias, gates)


def _mlstm_body(q_ref, k_ref, v_ref, o_ref, z_ref, grow_ref, col_ref, gain_ref, out_ref, state_ref, *, hd):
    L = A_CHUNK
    cb = pl.program_id(2)

    @pl.when(cb == 0)
    def _():
        state_ref[...] = jnp.zeros_like(state_ref)

    col = col_ref[0, 0, 0]
    row_i = lax.broadcasted_iota(jnp.int32, (L, L), 0)
    col_i = lax.broadcasted_iota(jnp.int32, (L, L), 1)
    causal = col_i <= row_i
    ones_b = jnp.ones((L, LANES), BF16)
    gain = gain_ref[0]

    for c in range(GATE_SLOTS):
        rows = slice(c * L, (c + 1) * L)

        def colvec(quantity):
            lane = quantity * GATE_SLOTS + c
            return col[:, lane:lane + 1]

        m_rel, inter_w, exp_neg_m, wa_scaled = colvec(0), colvec(1), colvec(2), colvec(3)
        decay = colvec(4)[0:1, :]
        g_row = grow_ref[0, 0, c:c + 1, :]

        q = q_ref[rows, :]
        k = k_ref[rows, :]
        v_aug = jnp.concatenate([v_ref[rows, :], ones_b], axis=1)

        dmat = jnp.exp(jnp.where(causal, g_row - m_rel, -jnp.inf))
        scores = lax.dot_general(q, k, (((1,), (1,)), ((), ())), preferred_element_type=F32) * dmat
        state = state_ref[...]
        total = (jnp.dot(scores.astype(BF16), v_aug, preferred_element_type=F32)
                 + inter_w * jnp.dot(q, state.astype(BF16), preferred_element_type=F32))
        num = total[:, :hd]
        den = total[:, hd:]
        inv = 1.0 / jnp.maximum(jnp.abs(den), exp_neg_m)
        h_tilde = num * jnp.concatenate([inv] * (hd // LANES), axis=1)

        gated2 = (jnp.tanh(o_ref[rows, :].astype(F32)) + 1.0) * h_tilde
        ms2 = jnp.mean(gated2 * gated2, axis=-1, keepdims=True)
        h_a = gated2 * lax.rsqrt(ms2 + 4.0 * NORM_EPS) * gain
        out_ref[rows, :] = (h_a * _silu_of_twice(z_ref[rows, :].astype(F32))).astype(BF16)

        kw = (k.astype(F32) * wa_scaled).astype(BF16)
        update = lax.dot_general(kw, v_aug, (((0,), (0,)), ((), ())), preferred_element_type=F32)
        state_ref[...] = decay * state + update


def _mlstm_chunks(qk, proj, grow, col, head_gain, *, batch, seq, d):
    hd = d // A_HEADS
    L = A_CHUNK
    rows = GATE_SLOTS * L
    blocks_per_seq = seq // rows
    heads_per_tile = d // hd

    def col_block(tile):
        return pl.BlockSpec((rows, hd), lambda b, h, cb: (b * blocks_per_seq + cb, tile * heads_per_tile + h))

    return pl.pallas_call(
        functools.partial(_mlstm_body, hd=hd),
        grid=(batch, A_HEADS, blocks_per_seq),
        in_specs=[
            col_block(T_QA), col_block(T_KA),
            col_block(T_VA - N_CONV), col_block(T_OA - N_CONV), col_block(T_ZA - N_CONV),
            pl.BlockSpec((1, 1, GATE_SLOTS, L), lambda b, h, cb: (b, h, cb, 0)),
            pl.BlockSpec((1, 1, 1, L, LANES), lambda b, h, cb: (b, h, cb, 0, 0)),
            pl.BlockSpec((1, 1, hd), lambda b, h, cb: (h, 0, 0)),
        ],
        out_specs=pl.BlockSpec((rows, hd), lambda b, h, cb: (b * blocks_per_seq + cb, h)),
        out_shape=jax.ShapeDtypeStruct((batch * seq, d), BF16),
        scratch_shapes=[pltpu.VMEM((hd, hd + LANES), F32)],
        compiler_params=pltpu.CompilerParams(
            dimension_semantics=("parallel", "parallel", "arbitrary"), vmem_limit_bytes=VMEM_LIMIT),
        name="mlstm_chunks",
    )(qk, qk, proj, proj, proj, grow, col, head_gain)


def _stickbreak_body(q_ref, k_ref, v_ref, out_ref, *, tq, heads_per_step):
    qi = pl.program_id(2)
    hd = B_HEAD_DIM
    q0 = pl.multiple_of(qi * tq, tq)

    row_i = lax.broadcasted_iota(jnp.int32, (tq, tq), 0)
    col_i = lax.broadcasted_iota(jnp.int32, (tq, tq), 1)
    strictly_before = col_i < row_i
    suffix_op = jnp.where(row_i > col_i, 1.0, 0.0).astype(BF16)

    def scores(q, head, k0, diagonal):
        kblk = k_ref[pl.ds(k0, tq), head * hd:(head + 1) * hd]
        z2 = lax.dot_general(q, kblk, (((1,), (1,)), ((), ())), preferred_element_type=F32)
        sp2 = jnp.maximum(z2, 0.0) + jnp.log(1.0 + jnp.exp2(-jnp.abs(z2))) * LOG2_E
        if diagonal:
            sp2 = jnp.where(strictly_before, sp2, 0.0)
        return z2, sp2

    def suffix_sums(sp2_list):
        stacked = jnp.concatenate([sp2.astype(BF16) for sp2 in sp2_list], axis=0)
        local = jnp.dot(stacked, suffix_op, preferred_element_type=F32)
        return [local[n * tq:(n + 1) * tq, :] for n in range(len(sp2_list))]

    def weights(z2, sp2, later, diagonal):
        w = jnp.exp2(z2 - sp2 - later)
        if diagonal:
            w = jnp.where(strictly_before, w, 0.0)
        return w.astype(BF16)

    def values(head, k0):
        return v_ref[pl.ds(k0, tq), head * hd:(head + 1) * hd]

    has_prev = jnp.where(qi > 0, 1.0, 0.0)
    has_prev_b = has_prev.astype(BF16)
    prev0 = pl.multiple_of(jnp.maximum(q0 - tq, 0), tq)
    qs = [q_ref[:, head * hd:(head + 1) * hd] for head in range(heads_per_step)]
    diag = [scores(qs[head], head, q0, True) for head in range(heads_per_step)]
    prev = [scores(qs[head], head, prev0, False) for head in range(heads_per_step)]
    local = suffix_sums([sp2 for _, sp2 in diag] + [sp2 for _, sp2 in prev])
    carries = []
    for head in range(heads_per_step):
        (z2_d, sp2_d), (z2_p, sp2_p) = diag[head], prev[head]
        total_d = jnp.sum(sp2_d, axis=-1, keepdims=True)
        w_both = jnp.concatenate([weights(z2_d, sp2_d, local[head], True),
                                  weights(z2_p, sp2_p, local[heads_per_step + head] + total_d, False)], axis=1)
        v_both = jnp.concatenate([values(head, q0), values(head, prev0) * has_prev_b], axis=0)
        consumed = total_d + jnp.sum(sp2_p, axis=-1, keepdims=True) * has_prev
        carries.append((consumed, jnp.dot(w_both, v_both, preferred_element_type=F32)))

    for head in range(heads_per_step):
        out_ref[:, head * hd:(head + 1) * hd] = carries[head][1].astype(BF16)

    least_consumed = functools.reduce(jnp.minimum, [consumed for consumed, _ in carries])

    @pl.when(jnp.logical_and(qi > 1, jnp.min(least_consumed) < STICK_EXHAUSTED_LOG2))
    def _():
        for head in range(heads_per_step):
            def more(state):
                it, consumed, _ = state
                return jnp.logical_and(it < qi - 1, jnp.min(consumed) < STICK_EXHAUSTED_LOG2)

            def step(state, head=head):
                it, consumed, acc = state
                k0 = pl.multiple_of((qi - 2 - it) * tq, tq)
                z2, sp2 = scores(qs[head], head, k0, False)
                later = suffix_sums([sp2])[0] + consumed
                acc = acc + jnp.dot(weights(z2, sp2, later, False), values(head, k0),
                                    preferred_element_type=F32)
                return it + 1, consumed + jnp.sum(sp2, axis=-1, keepdims=True), acc

            _, _, acc = lax.while_loop(more, step, (jnp.int32(0),) + carries[head])
            out_ref[:, head * hd:(head + 1) * hd] = acc.astype(BF16)


def _stickbreak(proj, *, batch, seq, d, tq, heads_per_step):
    hd = B_HEAD_DIM
    groups = d // (hd * heads_per_step)
    width = hd * heads_per_step
    nq = seq // tq
    return pl.pallas_call(
        functools.partial(_stickbreak_body, tq=tq, heads_per_step=heads_per_step),
        grid=(batch, groups, nq),
        in_specs=[
            pl.BlockSpec((tq, width), lambda b, g, qi: (b * nq + qi, (T_QB - N_CONV) * groups + g)),
            pl.BlockSpec((seq, width), lambda b, g, qi: (b, (T_KB - N_CONV) * groups + g)),
            pl.BlockSpec((seq, width), lambda b, g, qi: (b, (T_VB - N_CONV) * groups + g)),
        ],
        out_specs=pl.BlockSpec((tq, width), lambda b, g, qi: (b * nq + qi, g)),
        out_shape=jax.ShapeDtypeStruct((batch * seq, d), BF16),
        compiler_params=pltpu.CompilerParams(
            dimension_semantics=("parallel", "parallel", "arbitrary"), vmem_limit_bytes=VMEM_LIMIT),
        name="stickbreak",
    )(proj, proj, proj)


def _out_proj_body(x_ref, ha_ref, ob_ref, zb_ref, ga_ref, gb_ref, wa_ref, wb_ref, wo_ref, out_ref):
    ob = (ob_ref[...].astype(F32) * _silu_of_twice(zb_ref[...].astype(F32))).astype(BF16)
    y_a = jnp.dot(ha_ref[...], wa_ref[...], preferred_element_type=F32)
    y_b = jnp.dot(ob, wb_ref[...], preferred_element_type=F32)
    merged2 = ((jnp.tanh(ga_ref[...].astype(F32)) + 1.0) * y_a
               + (jnp.tanh(gb_ref[...].astype(F32)) + 1.0) * y_b)
    out_ref[...] = x_ref[...] + jnp.dot(merged2.astype(BF16), wo_ref[...], preferred_element_type=F32)


def _out_proj(x2d, ha, ob, proj, w_a, w_b, w_o_half, *, tm):
    m, d = x2d.shape
    row_tile = pl.BlockSpec((tm, d), lambda i: (i, 0))
    weight = pl.BlockSpec((d, d), lambda i: (0, 0))

    def proj_tile(tile):
        return pl.BlockSpec((tm, d), lambda i: (i, tile - N_CONV))

    return pl.pallas_call(
        _out_proj_body,
        grid=(m // tm,),
        in_specs=[row_tile, row_tile, row_tile, proj_tile(T_ZB), proj_tile(T_GA), proj_tile(T_GB),
                  weight, weight, weight],
        out_specs=row_tile,
        out_shape=jax.ShapeDtypeStruct((m, d), F32),
        compiler_params=pltpu.CompilerParams(
            dimension_semantics=("parallel",), vmem_limit_bytes=VMEM_LIMIT),
        name="out_proj",
    )(x2d, ha, ob, proj, proj, proj, w_a, w_b, w_o_half)


def _layer(x, norm_gain, w_in, conv_w, conv_b, i_bias, f_bias, a_head_gain, qk_gain_q, qk_gain_k,
           w_branch_a, w_branch_b, w_out):
    batch, seq, d = x.shape
    m = batch * seq
    nc = seq // A_CHUNK
    hd_a = d // A_HEADS
    b_heads = d // B_HEAD_DIM
    assert seq % (GATE_SLOTS * A_CHUNK) == 0 and d % LANES == 0

    gate_lo = 5 * d
    gate_hi = gate_lo + 2 * A_HEADS
    tile_scale = jnp.ones((N_TILES,), F32).at[jnp.array(HALVED_TILES)].set(0.5)
    w_bf = w_in.astype(BF16)
    w_main = jnp.concatenate([w_bf[:, :gate_lo], w_bf[:, gate_hi:]], axis=1)
    w_main = w_main * jnp.repeat(tile_scale, d).astype(BF16)[None, :]
    w_gate = jnp.pad(w_bf[:, gate_lo:gate_hi], ((0, 0), (0, LANES - 2 * A_HEADS)))
    qk_gain = jnp.stack([jnp.tile(qk_gain_q, b_heads) * (LOG2_E / math.sqrt(B_HEAD_DIM)),
                         jnp.tile(qk_gain_k, b_heads)])[:, None, :]
    conv_w_half = 0.5 * conv_w.reshape(CONV_WIDTH, N_CONV, d).transpose(1, 0, 2)
    conv_b_half = 0.5 * conv_b.reshape(N_CONV, 1, d)

    x2d = x.reshape(m, d)
    proj, qk, gates = _in_proj(x2d, norm_gain[None, :], w_main, w_gate, conv_w_half, conv_b_half, qk_gain,
                               seq=seq, tm=min(1024, seq))

    grow, col = _mlstm_gates(jnp.stack([i_bias, f_bias]), gates, batch=batch, nc=nc)
    ha = _mlstm_chunks(qk, proj, grow, col, a_head_gain.reshape(A_HEADS, 1, hd_a), batch=batch, seq=seq, d=d)

    ob = _stickbreak(proj, batch=batch, seq=seq, d=d, tq=256, heads_per_step=4)

    out = _out_proj(x2d, ha, ob, proj, w_branch_a.astype(BF16), w_branch_b.astype(BF16),
                    (0.5 * w_out).astype(BF16), tm=512)
    return out.reshape(batch, seq, d)


def kernel(x, norm_gain, w_in, conv_w, conv_b, i_bias, f_bias, a_head_gain, qk_gain_q, qk_gain_k,
           w_branch_a, w_branch_b, w_out):
    for layer in range(norm_gain.shape[0]):
        x = _layer(x, norm_gain[layer], w_in[layer], conv_w[layer], conv_b[layer], i_bias[layer],
                   f_bias[layer], a_head_gain[layer], qk_gain_q[layer], qk_gain_k[layer],
                   w_branch_a[layer], w_branch_b[layer], w_out[layer])
    return x
```

```python
import functools
import math

import jax
import jax.numpy as jnp
from jax import lax
from jax.experimental import pallas as pl
from jax.experimental.pallas import tpu as pltpu

F32 = jnp.float32
BF16 = jnp.bfloat16

NORM_EPS = 1e-6
A_HEADS = 4
A_CHUNK = 128
CONV_WIDTH = 4
B_HEAD_DIM = 128
LANES = 128
SUBLANES = 8
VMEM_LIMIT = 56 * 1024 * 1024

T_QA, T_KA, T_VA, T_OA, T_ZA, T_QB, T_KB, T_VB, T_ZB, T_GA, T_GB = range(11)
N_TILES = 11
N_CONV = 2
HALVED_TILES = (T_OA, T_ZA, T_ZB, T_GA, T_GB)
GATE_SLOTS = 16
LOG2_E = 1.0 / math.log(2.0)
STICK_EXHAUSTED_LOG2 = 160.0


def _silu_of_twice(u):
    return u * (jnp.tanh(u) + 1.0)


def _in_proj_body(x_ref, ng_ref, w_ref, wg_ref, cw_ref, cb_ref, qkg_ref, proj_ref, qk_ref, gates_ref,
                  hn_ref, halo_ref, *, tm, d, tiles_per_seq, rc, k_scale):
    i = pl.program_id(0)
    j = pl.program_id(1)
    first = (i % tiles_per_seq) == 0

    def project():
        return jnp.dot(hn_ref[...], w_ref[...], preferred_element_type=F32)

    def conv_silu(slot):
        acc = project()
        window = jnp.concatenate([jnp.where(first, 0.0, halo_ref[slot]), acc], axis=0)
        halo_ref[slot] = acc[tm - SUBLANES:, :]
        u = cw_ref[slot, 0:1, :] * window
        for tap in range(1, CONV_WIDTH):
            u = cw_ref[slot, tap:tap + 1, :] * window + pltpu.roll(u, 1, 0)
        y = _silu_of_twice(u[SUBLANES:, :] + cb_ref[slot])
        qk_ref[...] = (y * k_scale if slot == T_KA else y).astype(BF16)

    @pl.when(j == T_QA)
    def _():
        def norm_rows(r, carry):
            rows = pl.ds(pl.multiple_of(r * rc, rc), rc)
            xf = x_ref[rows, :]
            ms = jnp.mean(xf * xf, axis=-1, keepdims=True)
            hn_ref[rows, :] = (xf * lax.rsqrt(ms + NORM_EPS) * ng_ref[...]).astype(BF16)
            return carry
        lax.fori_loop(0, tm // rc, norm_rows, 0)
        gates_ref[...] = jnp.dot(hn_ref[...], wg_ref[...], preferred_element_type=F32)
        conv_silu(T_QA)

    @pl.when(j == T_KA)
    def _():
        conv_silu(T_KA)

    @pl.when(jnp.logical_or(j == T_QB, j == T_KB))
    def _():
        acc = project()
        for hh in range(d // B_HEAD_DIM):
            cols = slice(hh * B_HEAD_DIM, (hh + 1) * B_HEAD_DIM)
            blk = acc[:, cols]
            ms = jnp.mean(blk * blk, axis=-1, keepdims=True)
            proj_ref[:, cols] = (blk * lax.rsqrt(ms + NORM_EPS) * qkg_ref[0, :, cols]).astype(BF16)

    is_plain = functools.reduce(jnp.logical_or, [j == t for t in (T_VA, T_OA, T_ZA, T_VB, T_ZB, T_GA, T_GB)])

    @pl.when(is_plain)
    def _():
        proj_ref[...] = project().astype(BF16)


def _in_proj(x2d, norm_gain, w_main, w_gate, conv_w, conv_b, qk_gain, *, seq, tm):
    m, d = x2d.shape
    body = functools.partial(_in_proj_body, tm=tm, d=d, tiles_per_seq=seq // tm, rc=256,
                             k_scale=1.0 / math.sqrt(d // A_HEADS))
    return pl.pallas_call(
        body,
        grid=(m // tm, N_TILES),
        in_specs=[
            pl.BlockSpec((tm, d), lambda i, j: (i, 0)),
            pl.BlockSpec((1, d), lambda i, j: (0, 0)),
            pl.BlockSpec((d, d), lambda i, j: (0, j)),
            pl.BlockSpec((d, LANES), lambda i, j: (0, 0)),
            pl.BlockSpec((N_CONV, CONV_WIDTH, d), lambda i, j: (0, 0, 0)),
            pl.BlockSpec((N_CONV, 1, d), lambda i, j: (0, 0, 0)),
            pl.BlockSpec((1, 1, d), lambda i, j: (jnp.clip(j - T_QB, 0, 1), 0, 0)),
        ],
        out_specs=[
            pl.BlockSpec((tm, d), lambda i, j: (i, jnp.maximum(j - N_CONV, 0))),
            pl.BlockSpec((tm, d), lambda i, j: (i, jnp.minimum(j, N_CONV - 1))),
            pl.BlockSpec((tm, LANES), lambda i, j: (i, 0)),
        ],
        out_shape=[
            jax.ShapeDtypeStruct((m, (N_TILES - N_CONV) * d), BF16),
            jax.ShapeDtypeStruct((m, N_CONV * d), BF16),
            jax.ShapeDtypeStruct((m, LANES), F32),
        ],
        scratch_shapes=[
            pltpu.VMEM((tm, d), BF16),
            pltpu.VMEM((N_CONV, SUBLANES, d), F32),
        ],
        compiler_params=pltpu.CompilerParams(
            dimension_semantics=("arbitrary", "arbitrary"), vmem_limit_bytes=VMEM_LIMIT),
        name="in_proj",
    )(x2d, norm_gain, w_main, w_gate, conv_w, conv_b, qk_gain)


def _lane_scan(v, op, ident):
    lane = lax.broadcasted_iota(jnp.int32, v.shape, 1)
    shift = 1
    while shift < v.shape[1]:
        moved = pltpu.roll(v, shift, 1)
        v = op(v, jnp.where(lane >= shift, moved, ident))
        shift *= 2
    return v


def _gates_body(bias_ref, gates_ref, grow_ref, col_ref, lanes_ref, mprev_ref, *, nc):
    L = A_CHUNK
    n_gates = 2 * A_HEADS
    for c in range(nc):
        lanes_ref[c * n_gates:(c + 1) * n_gates, :] = gates_ref[c * L:(c + 1) * L, :].T[0:n_gates, :]
    for h in range(A_HEADS):
        _gates_one_head(bias_ref, lanes_ref, grow_ref, col_ref, mprev_ref, h, nc=nc)


def _gates_one_head(bias_ref, lanes_ref, grow_ref, col_ref, mprev_ref, h, *, nc):
    L = A_CHUNK
    n_gates = 2 * A_HEADS
    i_pre = lanes_ref[pl.ds(h, nc, stride=n_gates), :] + bias_ref[0, h]
    f_pre = lanes_ref[pl.ds(A_HEADS + h, nc, stride=n_gates), :] + bias_ref[1, h]
    logf = jnp.minimum(f_pre, 0.0) - jnp.log1p(jnp.exp(-jnp.abs(f_pre)))
    bcum = _lane_scan(logf, jnp.add, 0.0)
    b_last = jnp.broadcast_to(bcum[:, L - 1:L], (nc, L))
    a = b_last - bcum + i_pre
    a_max = jnp.broadcast_to(jnp.max(a, axis=-1, keepdims=True), (nc, L))
    wa = jnp.exp(a - a_max)
    g = i_pre - bcum
    g_cmax = _lane_scan(g, jnp.maximum, -jnp.inf)

    m_run = jnp.zeros((1, L), F32)
    for c in range(nc):
        mprev_ref[c:c + 1, :] = m_run
        m_run = jnp.maximum(b_last[c:c + 1, :] + m_run, a_max[c:c + 1, :])
    m_prev = mprev_ref[...]

    m_rel = jnp.maximum(m_prev, g_cmax)
    inter_w = jnp.exp(m_prev - m_rel)
    exp_neg_m = jnp.exp(-(bcum + m_rel))
    m_new = jnp.maximum(b_last + m_prev, a_max)
    decay = jnp.exp(b_last + m_prev - m_new)
    wa_scaled = wa * jnp.exp(a_max - m_new)

    grow_ref[0, h] = g
    zeros = jnp.zeros((GATE_SLOTS, L), F32)
    for blk in range(nc // GATE_SLOTS):
        rows = slice(blk * GATE_SLOTS, (blk + 1) * GATE_SLOTS)
        packed = jnp.concatenate(
            [m_rel[rows], inter_w[rows], exp_neg_m[rows], wa_scaled[rows], decay[rows], zeros, zeros, zeros],
            axis=0)
        col_ref[0, h, blk] = packed.T


def _mlstm_gates(bias, gates, *, batch, nc):
    L = A_CHUNK
    return pl.pallas_call(
        functools.partial(_gates_body, nc=nc),
        grid=(batch,),
        in_specs=[
            pl.BlockSpec(memory_space=pltpu.SMEM),
            pl.BlockSpec((nc * L, LANES), lambda bi: (bi, 0)),
        ],
        out_specs=[
            pl.BlockSpec((1, A_HEADS, nc, L), lambda bi: (bi, 0, 0, 0)),
            pl.BlockSpec((1, A_HEADS, nc // GATE_SLOTS, L, LANES), lambda bi: (bi, 0, 0, 0, 0)),
        ],
        out_shape=[
            jax.ShapeDtypeStruct((batch, A_HEADS, nc, L), F32),
            jax.ShapeDtypeStruct((batch, A_HEADS, nc // GATE_SLOTS, L, LANES), F32),
        ],
        scratch_shapes=[pltpu.VMEM((nc * 2 * A_HEADS, L), F32), pltpu.VMEM((nc, L), F32)],
        compiler_params=pltpu.CompilerParams(dimension_semantics=("parallel",)),
        name="mlstm_gates",
    )(bias, gates)


def _mlstm_body(q_ref, k_ref, v_ref, o_ref, z_ref, grow_ref, col_ref, gain_ref, out_ref, state_ref, *,
                hd, heads_per_step):
    L = A_CHUNK
    cb = pl.program_id(2)

    @pl.when(cb == 0)
    def _():
        state_ref[...] = jnp.zeros_like(state_ref)

    row_i = lax.broadcasted_iota(jnp.int32, (L, L), 0)
    col_i = lax.broadcasted_iota(jnp.int32, (L, L), 1)
    causal = col_i <= row_i
    ones_b = jnp.ones((L, LANES), BF16)

    for c in range(GATE_SLOTS):
        rows = slice(c * L, (c + 1) * L)
        for hh in range(heads_per_step):
            lanes = slice(hh * hd, (hh + 1) * hd)

            def colvec(quantity):
                lane = quantity * GATE_SLOTS + c
                return col_ref[0, hh, 0, :, lane:lane + 1]

            m_rel, inter_w, exp_neg_m, wa_scaled = colvec(0), colvec(1), colvec(2), colvec(3)
            decay = colvec(4)[0:1, :]
            g_row = grow_ref[0, hh, c:c + 1, :]

            q = q_ref[rows, lanes]
            k = k_ref[rows, lanes]
            v_aug = jnp.concatenate([v_ref[rows, lanes], ones_b], axis=1)

            dmat = jnp.exp(jnp.where(causal, g_row - m_rel, -jnp.inf))
            scores = lax.dot_general(q, k, (((1,), (1,)), ((), ())), preferred_element_type=F32) * dmat
            state = state_ref[hh]
            total = (jnp.dot(scores.astype(BF16), v_aug, preferred_element_type=F32)
                     + inter_w * jnp.dot(q, state.astype(BF16), preferred_element_type=F32))
            num = total[:, :hd]
            den = total[:, hd:]
            inv = 1.0 / jnp.maximum(jnp.abs(den), exp_neg_m)
            h_tilde = num * jnp.concatenate([inv] * (hd // LANES), axis=1)

            gated2 = (jnp.tanh(o_ref[rows, lanes].astype(F32)) + 1.0) * h_tilde
            ms2 = jnp.mean(gated2 * gated2, axis=-1, keepdims=True)
            h_a = gated2 * lax.rsqrt(ms2 + 4.0 * NORM_EPS) * gain_ref[hh]
            out_ref[rows, lanes] = (h_a * _silu_of_twice(z_ref[rows, lanes].astype(F32))).astype(BF16)

            kw = (k.astype(F32) * wa_scaled).astype(BF16)
            update = lax.dot_general(kw, v_aug, (((0,), (0,)), ((), ())), preferred_element_type=F32)
            state_ref[hh] = decay * state + update


def _mlstm_chunks(qk, proj, grow, col, head_gain, *, batch, seq, d, heads_per_step):
    hd = d // A_HEADS
    L = A_CHUNK
    rows = GATE_SLOTS * L
    blocks_per_seq = seq // rows
    groups = A_HEADS // heads_per_step
    width = hd * heads_per_step

    def col_block(tile):
        return pl.BlockSpec((rows, width), lambda b, g, cb: (b * blocks_per_seq + cb, tile * groups + g))

    return pl.pallas_call(
        functools.partial(_mlstm_body, hd=hd, heads_per_step=heads_per_step),
        grid=(batch, groups, blocks_per_seq),
        in_specs=[
            col_block(T_QA), col_block(T_KA),
            col_block(T_VA - N_CONV), col_block(T_OA - N_CONV), col_block(T_ZA - N_CONV),
            pl.BlockSpec((1, heads_per_step, GATE_SLOTS, L), lambda b, g, cb: (b, g, cb, 0)),
            pl.BlockSpec((1, heads_per_step, 1, L, LANES), lambda b, g, cb: (b, g, cb, 0, 0)),
            pl.BlockSpec((heads_per_step, 1, hd), lambda b, g, cb: (g, 0, 0)),
        ],
        out_specs=pl.BlockSpec((rows, width), lambda b, g, cb: (b * blocks_per_seq + cb, g)),
        out_shape=jax.ShapeDtypeStruct((batch * seq, d), BF16),
        scratch_shapes=[pltpu.VMEM((heads_per_step, hd, hd + LANES), F32)],
        compiler_params=pltpu.CompilerParams(
            dimension_semantics=("parallel", "parallel", "arbitrary"), vmem_limit_bytes=VMEM_LIMIT),
        name="mlstm_chunks",
    )(qk, qk, proj, proj, proj, grow, col, head_gain)


def _stickbreak_body(q_ref, k_ref, v_ref, out_ref, *, tq, heads_per_step):
    qi = pl.program_id(2)
    hd = B_HEAD_DIM
    q0 = pl.multiple_of(qi * tq, tq)

    row_i = lax.broadcasted_iota(jnp.int32, (tq, tq), 0)
    col_i = lax.broadcasted_iota(jnp.int32, (tq, tq), 1)
    strictly_before = col_i < row_i
    suffix_op = jnp.where(row_i > col_i, 1.0, 0.0).astype(BF16)

    def scores(q, head, k0, diagonal):
        kblk = k_ref[pl.ds(k0, tq), head * hd:(head + 1) * hd]
        z2 = lax.dot_general(q, kblk, (((1,), (1,)), ((), ())), preferred_element_type=F32)
        sp2 = jnp.maximum(z2, 0.0) + jnp.log(1.0 + jnp.exp2(-jnp.abs(z2))) * LOG2_E
        if diagonal:
            sp2 = jnp.where(strictly_before, sp2, 0.0)
        return z2, sp2

    def suffix_sums(sp2_list):
        stacked = jnp.concatenate([sp2.astype(BF16) for sp2 in sp2_list], axis=0)
        local = jnp.dot(stacked, suffix_op, preferred_element_type=F32)
        return [local[n * tq:(n + 1) * tq, :] for n in range(len(sp2_list))]

    def weights(z2, sp2, later, diagonal):
        w = jnp.exp2(z2 - sp2 - later)
        if diagonal:
            w = jnp.where(strictly_before, w, 0.0)
        return w.astype(BF16)

    def values(head, k0):
        return v_ref[pl.ds(k0, tq), head * hd:(head + 1) * hd]

    has_prev = jnp.where(qi > 0, 1.0, 0.0)
    has_prev_b = has_prev.astype(BF16)
    prev0 = pl.multiple_of(jnp.maximum(q0 - tq, 0), tq)
    qs = [q_ref[:, head * hd:(head + 1) * hd] for head in range(heads_per_step)]
    diag = [scores(qs[head], head, q0, True) for head in range(heads_per_step)]
    prev = [scores(qs[head], head, prev0, False) for head in range(heads_per_step)]
    local = suffix_sums([sp2 for _, sp2 in diag] + [sp2 for _, sp2 in prev])
    carries = []
    for head in range(heads_per_step):
        (z2_d, sp2_d), (z2_p, sp2_p) = diag[head], prev[head]
        total_d = jnp.sum(sp2_d, axis=-1, keepdims=True)
        w_both = jnp.concatenate([weights(z2_d, sp2_d, local[head], True),
                                  weights(z2_p, sp2_p, local[heads_per_step + head] + total_d, False)], axis=1)
        v_both = jnp.concatenate([values(head, q0), values(head, prev0) * has_prev_b], axis=0)
        consumed = total_d + jnp.sum(sp2_p, axis=-1, keepdims=True) * has_prev
        carries.append((consumed, jnp.dot(w_both, v_both, preferred_element_type=F32)))

    for head in range(heads_per_step):
        out_ref[:, head * hd:(head + 1) * hd] = carries[head][1].astype(BF16)

    least_consumed = functools.reduce(jnp.minimum, [consumed for consumed, _ in carries])

    @pl.when(jnp.logical_and(qi > 1, jnp.min(least_consumed) < STICK_EXHAUSTED_LOG2))
    def _():
        for head in range(heads_per_step):
            def more(state):
                it, consumed, _ = state
                return jnp.logical_and(it < qi - 1, jnp.min(consumed) < STICK_EXHAUSTED_LOG2)

            def step(state, head=head):
                it, consumed, acc = state
                k0 = pl.multiple_of((qi - 2 - it) * tq, tq)
                z2, sp2 = scores(qs[head], head, k0, False)
                later = suffix_sums([sp2])[0] + consumed
                acc = acc + jnp.dot(weights(z2, sp2, later, False), values(head, k0),
                                    preferred_element_type=F32)
                return it + 1, consumed + jnp.sum(sp2, axis=-1, keepdims=True), acc

            _, _, acc = lax.while_loop(more, step, (jnp.int32(0),) + carries[head])
            out_ref[:, head * hd:(head + 1) * hd] = acc.astype(BF16)


def _stickbreak(proj, *, batch, seq, d, tq, heads_per_step):
    hd = B_HEAD_DIM
    groups = d // (hd * heads_per_step)
    width = hd * heads_per_step
    nq = seq // tq
    return pl.pallas_call(
        functools.partial(_stickbreak_body, tq=tq, heads_per_step=heads_per_step),
        grid=(batch, groups, nq),
        in_specs=[
            pl.BlockSpec((tq, width), lambda b, g, qi: (b * nq + qi, (T_QB - N_CONV) * groups + g)),
            pl.BlockSpec((seq, width), lambda b, g, qi: (b, (T_KB - N_CONV) * groups + g)),
            pl.BlockSpec((seq, width), lambda b, g, qi: (b, (T_VB - N_CONV) * groups + g)),
        ],
        out_specs=pl.BlockSpec((tq, width), lambda b, g, qi: (b * nq + qi, g)),
        out_shape=jax.ShapeDtypeStruct((batch * seq, d), BF16),
        compiler_params=pltpu.CompilerParams(
            dimension_semantics=("parallel", "parallel", "arbitrary"), vmem_limit_bytes=VMEM_LIMIT),
        name="stickbreak",
    )(proj, proj, proj)


def _out_proj_body(x_ref, ha_ref, ob_ref, zb_ref, ga_ref, gb_ref, wa_ref, wb_ref, wo_ref, out_ref):
    ob = (ob_ref[...].astype(F32) * _silu_of_twice(zb_ref[...].astype(F32))).astype(BF16)
    y_a = jnp.dot(ha_ref[...], wa_ref[...], preferred_element_type=F32)
    y_b = jnp.dot(ob, wb_ref[...], preferred_element_type=F32)
    merged2 = ((jnp.tanh(ga_ref[...].astype(F32)) + 1.0) * y_a
               + (jnp.tanh(gb_ref[...].astype(F32)) + 1.0) * y_b)
    out_ref[...] = x_ref[...] + jnp.dot(merged2.astype(BF16), wo_ref[...], preferred_element_type=F32)


def _out_proj(x2d, ha, ob, proj, w_a, w_b, w_o_half, *, tm):
    m, d = x2d.shape
    row_tile = pl.BlockSpec((tm, d), lambda i: (i, 0))
    weight = pl.BlockSpec((d, d), lambda i: (0, 0))

    def proj_tile(tile):
        return pl.BlockSpec((tm, d), lambda i: (i, tile - N_CONV))

    return pl.pallas_call(
        _out_proj_body,
        grid=(m // tm,),
        in_specs=[row_tile, row_tile, row_tile, proj_tile(T_ZB), proj_tile(T_GA), proj_tile(T_GB),
                  weight, weight, weight],
        out_specs=row_tile,
        out_shape=jax.ShapeDtypeStruct((m, d), F32),
        compiler_params=pltpu.CompilerParams(
            dimension_semantics=("parallel",), vmem_limit_bytes=VMEM_LIMIT),
        name="out_proj",
    )(x2d, ha, ob, proj, proj, proj, w_a, w_b, w_o_half)


def _layer(x, norm_gain, w_in, conv_w, conv_b, i_bias, f_bias, a_head_gain, qk_gain_q, qk_gain_k,
           w_branch_a, w_branch_b, w_out):
    batch, seq, d = x.shape
    m = batch * seq
    nc = seq // A_CHUNK
    hd_a = d // A_HEADS
    b_heads = d // B_HEAD_DIM
    assert seq % (GATE_SLOTS * A_CHUNK) == 0 and d % LANES == 0

    gate_lo = 5 * d
    gate_hi = gate_lo + 2 * A_HEADS
    tile_scale = jnp.ones((N_TILES,), F32).at[jnp.array(HALVED_TILES)].set(0.5)
    w_bf = w_in.astype(BF16)
    w_main = jnp.concatenate([w_bf[:, :gate_lo], w_bf[:, gate_hi:]], axis=1)
    w_main = w_main * jnp.repeat(tile_scale, d).astype(BF16)[None, :]
    w_gate = jnp.pad(w_bf[:, gate_lo:gate_hi], ((0, 0), (0, LANES - 2 * A_HEADS)))
    qk_gain = jnp.stack([jnp.tile(qk_gain_q, b_heads) * (LOG2_E / math.sqrt(B_HEAD_DIM)),
                         jnp.tile(qk_gain_k, b_heads)])[:, None, :]
    conv_w_half = 0.5 * conv_w.reshape(CONV_WIDTH, N_CONV, d).transpose(1, 0, 2)
    conv_b_half = 0.5 * conv_b.reshape(N_CONV, 1, d)

    x2d = x.reshape(m, d)
    proj, qk, gates = _in_proj(x2d, norm_gain[None, :], w_main, w_gate, conv_w_half, conv_b_half, qk_gain,
                               seq=seq, tm=min(1024, seq))

    grow, col = _mlstm_gates(jnp.stack([i_bias, f_bias]), gates, batch=batch, nc=nc)
    ha = _mlstm_chunks(qk, proj, grow, col, a_head_gain.reshape(A_HEADS, 1, hd_a), batch=batch, seq=seq, d=d,
                       heads_per_step=2)

    ob = _stickbreak(proj, batch=batch, seq=seq, d=d, tq=256, heads_per_step=4)

    out = _out_proj(x2d, ha, ob, proj, w_branch_a.astype(BF16), w_branch_b.astype(BF16),
                    (0.5 * w_out).astype(BF16), tm=512)
    return out.reshape(batch, seq, d)


def kernel(x, norm_gain, w_in, conv_w, conv_b, i_bias, f_bias, a_head_gain, qk_gain_q, qk_gain_k,
           w_branch_a, w_branch_b, w_out):
    for layer in range(norm_gain.shape[0]):
        x = _layer(x, norm_gain[layer], w_in[layer], conv_w[layer], conv_b[layer], i_bias[layer],
                   f_bias[layer], a_head_gain[layer], qk_gain_q[layer], qk_gain_k[layer],
                   w_branch_a[layer], w_branch_b[layer], w_out[layer])
    return x
```
